```python
import jax, jax.numpy as jnp
from jax import lax
import numpy as np

D_MODEL = 2048
BATCH = 16
SEQ = 256
DEPTH = 4
DEC_BATCH = 8
DEC_SEQ = 2048
PAST_LEN = 512

GRID_W = 64
HEAD_DIM = 128
N_HEADS_A = 8
D_A = N_HEADS_A * HEAD_DIM
N_GROUPS_B = 8
GROUP_B = 128
D_B = N_GROUPS_B * GROUP_B
CHUNK = 128
WIN_ROWS_MAX = 8
WIN_COLS = 16
Q_BLOCK = 128
D_CONV = D_MODEL
CONV_WIDTH = 31
D_FF = 5504
N_EXPERTS = 8
TOP_K = 2
D_FF_EXPERT = 7168
N_EVEN = (DEPTH + 1) // 2
N_ODD = DEPTH // 2
EPS = 1e-6
NEG_INF = -1e30
ATTN_SCALE = HEAD_DIM ** -0.5

kernel_name = "hybrid_dit_natten_sgu_conformer_moe_step"


def _rms_norm(x, g):
    xf = x.astype(jnp.float32)
    y = xf * lax.rsqrt(jnp.mean(xf * xf, axis=-1, keepdims=True) + EPS)
    return (y * g.astype(jnp.float32)).astype(x.dtype)


def _layer_norm(x, g, b):
    xf = x.astype(jnp.float32)
    mu = jnp.mean(xf, axis=-1, keepdims=True)
    var = jnp.mean(jnp.square(xf - mu), axis=-1, keepdims=True)
    y = (xf - mu) * lax.rsqrt(var + EPS)
    return (y * g.astype(jnp.float32) + b.astype(jnp.float32)).astype(x.dtype)


def _modulate(x, g, shift, scale):
    return _rms_norm(x, g) * (1 + scale[:, None, :]) + shift[:, None, :]


def _even_projection(h, w_in, q_g, k_g):
    b_, l_ = h.shape[:2]
    p = h @ w_in
    q, k, v, u, vg = jnp.split(p, [D_A, 2 * D_A, 3 * D_A, 3 * D_A + D_B], axis=-1)
    q = _rms_norm(q.reshape(b_, l_, N_HEADS_A, HEAD_DIM), q_g)
    k = _rms_norm(k.reshape(b_, l_, N_HEADS_A, HEAD_DIM), k_g)
    v = v.reshape(b_, l_, N_HEADS_A, HEAD_DIM)
    return q, k, v, u, vg


def _context_attention(q, k, v):
    b_, s_ = q.shape[:2]
    nb = s_ // Q_BLOCK
    qb = q.reshape(b_, nb, Q_BLOCK, N_HEADS_A, HEAD_DIM).transpose(1, 0, 2, 3, 4)

    def block(qi):
        s = jnp.einsum('bqhd,bkhd->bhqk', qi, k).astype(jnp.float32) * ATTN_SCALE
        p = jax.nn.softmax(s, axis=-1).astype(v.dtype)
        return jnp.einsum('bhqk,bkhd->bqhd', p, v)

    o = lax.map(block, qb)
    return o.transpose(1, 0, 2, 3, 4).reshape(b_, s_, D_A)


def _neighbourhood_attention(q, k, v, ck, cv, rpb):
    b_, l_ = q.shape[:2]
    rows = l_ // GRID_W
    kh = min(WIN_ROWS_MAX, rows)
    r = jnp.arange(rows)
    col = jnp.arange(GRID_W)
    row_start = jnp.clip(r - kh // 2, 0, rows - kh)
    row_idx = row_start[:, None] + jnp.arange(kh)[None, :]
    col_start = jnp.clip(col - WIN_COLS // 2, 0, GRID_W - WIN_COLS)
    col_in = (col[None, :] >= col_start[:, None]) & (col[None, :] < col_start[:, None] + WIN_COLS)
    dr = row_idx - r[:, None] + (WIN_ROWS_MAX - 1)
    dc = jnp.clip(col[None, :] - col[:, None], -(WIN_COLS - 1), WIN_COLS - 1) + (WIN_COLS - 1)
    bias = rpb[:, dr[:, None, :, None], dc[None, :, None, :]].astype(jnp.float32)
    bias = jnp.where(col_in[None, None, :, None, :], bias, NEG_INF).transpose(1, 0, 2, 3, 4)
    qg = q.reshape(b_, rows, GRID_W, N_HEADS_A, HEAD_DIM)
    kb = k.reshape(b_, rows, GRID_W, N_HEADS_A, HEAD_DIM)[:, row_idx]
    vb = v.reshape(b_, rows, GRID_W, N_HEADS_A, HEAD_DIM)[:, row_idx]
    s_loc = jnp.einsum('brqhd,brikhd->brhqik', qg, kb).astype(jnp.float32) * ATTN_SCALE + bias[None]
    s_ctx = jnp.einsum('brqhd,bphd->brhqp', qg, ck).astype(jnp.float32) * ATTN_SCALE
    n_loc = kh * GRID_W
    s = jnp.concatenate([s_loc.reshape(b_, rows, N_HEADS_A, GRID_W, n_loc), s_ctx], axis=-1)
    p = jax.nn.softmax(s, axis=-1).astype(v.dtype)
    p_loc = p[..., :n_loc].reshape(b_, rows, N_HEADS_A, GRID_W, kh, GRID_W)
    p_ctx = p[..., n_loc:]
    o = (jnp.einsum('brhqik,brikhd->brqhd', p_loc, vb)
         + jnp.einsum('brhqp,bphd->brqhd', p_ctx, cv))
    return o.reshape(b_, l_, D_A)


def _spatial_gating(u, vg, ln_g, ln_b, w_s, b_s):
    b_, l_ = u.shape[:2]
    u = jax.nn.gelu(u)
    vg = _layer_norm(jax.nn.gelu(vg), ln_g, ln_b)
    vc = vg.reshape(b_, l_ // CHUNK, CHUNK, N_GROUPS_B, GROUP_B)
    s = jnp.einsum('gpq,bnqgc->bnpgc', w_s, vc) + b_s.T[None, None, :, :, None]
    return u * s.reshape(b_, l_, D_B)


def _conformer_conv(h, w1, b1, dw, dw_b, ln_g, ln_b, w2, b2):
    a, gt = jnp.split(h @ w1 + b1, 2, axis=-1)
    z = a * jax.nn.sigmoid(gt)
    z = lax.conv_general_dilated(
        z, dw.astype(z.dtype)[:, None, :], window_strides=(1,),
        padding=[(CONV_WIDTH // 2, CONV_WIDTH // 2)],
        dimension_numbers=('NWC', 'WIO', 'NWC'), feature_group_count=D_CONV) + dw_b
    z = jax.nn.silu(_layer_norm(z, ln_g, ln_b))
    return z @ w2 + b2


def _swiglu(h, wg, wu, wd):
    return (jax.nn.silu(h @ wg) * (h @ wu)) @ wd


def _moe(h, router_w, wg, wu, wd):
    b_, l_, d_ = h.shape
    t = h.reshape(b_ * l_, d_)
    logits = (t @ router_w).astype(jnp.float32)
    top_v, top_i = lax.top_k(logits, TOP_K)
    gates = jax.nn.softmax(top_v, axis=-1)
    combine = jnp.sum(jax.nn.one_hot(top_i, N_EXPERTS, dtype=jnp.float32) * gates[..., None], axis=1).astype(h.dtype)
    y = jnp.zeros_like(t)
    for e in range(N_EXPERTS):
        y = y + combine[:, e:e + 1] * _swiglu(t, wg[e], wu[e], wd[e])
    return y.reshape(b_, l_, d_)


def setup_inputs(seed: int = 0) -> dict:
    key = jax.random.key(seed)
    ks = iter(jax.random.split(key, 48))
    D = D_MODEL

    def nrm(shape, scale):
        return jax.random.normal(next(ks), shape, jnp.float32) * scale

    def gain(shape):
        return 1.0 + nrm(shape, 0.02)

    inp = {}
    inp['x_prompt'] = nrm((BATCH, SEQ, D), 1.0)
    inp['x_sample'] = nrm((DEC_BATCH, DEC_SEQ, D), 1.0)
    inp['cache_k'] = nrm((DEC_BATCH, N_EVEN, PAST_LEN, N_HEADS_A, HEAD_DIM), 1.0)
    inp['cache_v'] = nrm((DEC_BATCH, N_EVEN, PAST_LEN, N_HEADS_A, HEAD_DIM), 1.0)
    inp['c'] = nrm((DEC_BATCH, D), 1.0)
    inp['c_ctx'] = nrm((D,), 1.0)
    inp['w_mod'] = nrm((DEPTH, D, 6 * D), 0.5 * D ** -0.5)
    inp['b_mod'] = nrm((DEPTH, 6 * D), 0.02)
    inp['norm1_g'] = gain((DEPTH, D))
    inp['norm2_g'] = gain((DEPTH, D))
    inp['w_in_even'] = nrm((N_EVEN, D, 3 * D_A + 2 * D_B), D ** -0.5)
    inp['q_norm_g'] = gain((N_EVEN, HEAD_DIM))
    inp['k_norm_g'] = gain((N_EVEN, HEAD_DIM))
    inp['rpb'] = nrm((N_EVEN, N_HEADS_A, 2 * WIN_ROWS_MAX - 1, 2 * WIN_COLS - 1), 0.5)
    inp['sgu_ln_g'] = gain((N_EVEN, D_B))
    inp['sgu_ln_b'] = nrm((N_EVEN, D_B), 0.02)
    inp['sgu_w'] = nrm((N_EVEN, N_GROUPS_B, CHUNK, CHUNK), CHUNK ** -0.5)
    inp['sgu_b'] = nrm((N_EVEN, N_GROUPS_B, CHUNK), 0.02)
    inp['w_out_even'] = nrm((N_EVEN, D_A + D_B, D), (D_A + D_B) ** -0.5)
    inp['ffn_w_gate'] = nrm((N_EVEN, D, D_FF), D ** -0.5)
    inp['ffn_w_up'] = nrm((N_EVEN, D, D_FF), D ** -0.5)
    inp['ffn_w_down'] = nrm((N_EVEN, D_FF, D), D_FF ** -0.5)
    inp['conv_w1'] = nrm((N_ODD, D, 2 * D_CONV), D ** -0.5)
    inp['conv_b1'] = nrm((N_ODD, 2 * D_CONV), 0.02)
    inp['conv_dw'] = nrm((N_ODD, CONV_WIDTH, D_CONV), CONV_WIDTH ** -0.5)
    inp['conv_dw_b'] = nrm((N_ODD, D_CONV), 0.02)
    inp['conv_ln_g'] = gain((N_ODD, D_CONV))
    inp['conv_ln_b'] = nrm((N_ODD, D_CONV), 0.02)
    inp['conv_w2'] = nrm((N_ODD, D_CONV, D), D_CONV ** -0.5)
    inp['conv_b2'] = nrm((N_ODD, D), 0.02)
    inp['router_w'] = nrm((N_ODD, D, N_EXPERTS), D ** -0.5)
    inp['moe_w_gate'] = nrm((N_ODD, N_EXPERTS, D, D_FF_EXPERT), D ** -0.5)
    inp['moe_w_up'] = nrm((N_ODD, N_EXPERTS, D, D_FF_EXPERT), D ** -0.5)
    inp['moe_w_down'] = nrm((N_ODD, N_EXPERTS, D_FF_EXPERT, D), D_FF_EXPERT ** -0.5)
    return inp


def reference(x_prompt, x_sample, cache_k, cache_v, c, c_ctx, w_mod, b_mod, norm1_g, norm2_g,
              w_in_even, q_norm_g, k_norm_g, rpb, sgu_ln_g, sgu_ln_b, sgu_w, sgu_b, w_out_even,
              ffn_w_gate, ffn_w_up, ffn_w_down, conv_w1, conv_b1, conv_dw, conv_dw_b, conv_ln_g,
              conv_ln_b, conv_w2, conv_b2, router_w, moe_w_gate, moe_w_up, moe_w_down):
    xp, xs = x_prompt, x_sample
    sc_ctx = jax.nn.silu(c_ctx)[None, :]
    sc_lat = jax.nn.silu(c)
    new_k, new_v = [], []
    for l in range(DEPTH):
        mp = jnp.split(sc_ctx @ w_mod[l] + b_mod[l], 6, axis=-1)
        ms = jnp.split(sc_lat @ w_mod[l] + b_mod[l], 6, axis=-1)
        hp = _modulate(xp, norm1_g[l], mp[0], mp[1])
        hs = _modulate(xs, norm1_g[l], ms[0], ms[1])
        if l % 2 == 0:
            e = l // 2
            sgu = (sgu_ln_g[e], sgu_ln_b[e], sgu_w[e], sgu_b[e])
            qp, kp, vp, up, gp = _even_projection(hp, w_in_even[e], q_norm_g[e], k_norm_g[e])
            yp = jnp.concatenate([_context_attention(qp, kp, vp), _spatial_gating(up, gp, *sgu)], axis=-1) @ w_out_even[e]
            new_k.append(kp)
            new_v.append(vp)
            qs, ks_, vs, us, gs = _even_projection(hs, w_in_even[e], q_norm_g[e], k_norm_g[e])
            a_s = _neighbourhood_attention(qs, ks_, vs, cache_k[:, e], cache_v[:, e], rpb[e])
            ys = jnp.concatenate([a_s, _spatial_gating(us, gs, *sgu)], axis=-1) @ w_out_even[e]
        else:
            o = l // 2
            cargs = (conv_w1[o], conv_b1[o], conv_dw[o], conv_dw_b[o], conv_ln_g[o], conv_ln_b[o], conv_w2[o], conv_b2[o])
            yp = _conformer_conv(hp, *cargs)
            ys = _conformer_conv(hs, *cargs)
        xp = xp + mp[2][:, None, :] * yp
        xs = xs + ms[2][:, None, :] * ys
        hp = _modulate(xp, norm2_g[l], mp[3], mp[4])
        hs = _modulate(xs, norm2_g[l], ms[3], ms[4])
        if l % 2 == 0:
            e = l // 2
            yp = _swiglu(hp, ffn_w_gate[e], ffn_w_up[e], ffn_w_down[e])
            ys = _swiglu(hs, ffn_w_gate[e], ffn_w_up[e], ffn_w_down[e])
        else:
            o = l // 2
            yp = _moe(hp, router_w[o], moe_w_gate[o], moe_w_up[o], moe_w_down[o])
            ys = _moe(hs, router_w[o], moe_w_gate[o], moe_w_up[o], moe_w_down[o])
        xp = xp + mp[5][:, None, :] * yp
        xs = xs + ms[5][:, None, :] * ys
    new_cache_k = jnp.stack(new_k, axis=1)
    new_cache_v = jnp.stack(new_v, axis=1)
    return (xp, xs, new_cache_k, new_cache_v)
```

```python
import functools

import jax
import jax.numpy as jnp
from jax import lax
from jax.experimental import pallas as pl
from jax.experimental.pallas import tpu as pltpu

F32 = jnp.float32
BF16 = jnp.bfloat16
I32 = jnp.int32

D_MODEL = 2048
N_HEADS = 8
HEAD_DIM = 128
D_A = N_HEADS * HEAD_DIM
N_GROUPS = 8
GROUP = 128
D_B = N_GROUPS * GROUP
CHUNK = 128
GRID_W = 64
WIN_ROWS = 8
WIN_COLS = 16
CONV_W = 31
CONV_PAD = 16
N_EXPERTS = 8
EPS = 1e-6
NEG_INF = -1e30
ATTN_SCALE = HEAD_DIM ** -0.5
N_MOD_ROWS = 16
LANE = 128
D_FF_PAD = 5632

VMEM_LIMIT = 60 * 1024 * 1024

TM = 1024
TN = 1024
TM_FFN = 512
TF_FFN = 512
TF_MOE = 1024
TM_ROUTE = 512
TM_MOVE = 256
ROW_CHUNK = 32


def _params(*sem):
    return pltpu.CompilerParams(dimension_semantics=sem, vmem_limit_bytes=VMEM_LIMIT)


def _modulate(x, g, shift, scale):
    ms = jnp.mean(x * x, axis=-1, keepdims=True)
    return ((x * lax.rsqrt(ms + EPS)) * g) * (1.0 + scale) + shift


def _fill_modulated(x_ref, g_ref, sh_ref, sc_ref, h_ref):
    g, sh, sc = g_ref[...], sh_ref[...], sc_ref[...]

    def body(c, carry):
        r0 = pl.multiple_of(c * ROW_CHUNK, ROW_CHUNK)
        x = x_ref[pl.ds(r0, ROW_CHUNK), :]
        h_ref[pl.ds(r0, ROW_CHUNK), :] = _modulate(x, g, sh, sc).astype(h_ref.dtype)
        return carry

    lax.fori_loop(0, x_ref.shape[0] // ROW_CHUNK, body, 0)


def _mod_spec(layer, which, row_of, width=D_MODEL, col_of=None):
    if col_of is None:
        return pl.BlockSpec((None, None, None, 1, width), lambda i, *_: (layer, row_of(i), which, 0, 0))
    return pl.BlockSpec((None, None, None, 1, width), lambda i, j, *_: (layer, row_of(i), which, 0, col_of(j)))


def _mod_kernel(c_ref, w_ref, b_ref, o_ref):
    c = c_ref[...]
    sc = (c * jax.nn.sigmoid(c)).astype(BF16)
    o_ref[...] = jnp.dot(sc, w_ref[...].astype(BF16), preferred_element_type=F32) + b_ref[...]


def _modulation(cvec, w_mod, b_mod):
    depth, d, n = w_mod.shape
    tn = 1024
    out = pl.pallas_call(
        _mod_kernel,
        grid=(depth, n // tn),
        in_specs=[
            pl.BlockSpec((N_MOD_ROWS, d), lambda l, j: (0, 0)),
            pl.BlockSpec((None, d, tn), lambda l, j: (l, 0, j)),
            pl.BlockSpec((None, 1, tn), lambda l, j: (l, 0, j)),
        ],
        out_specs=pl.BlockSpec((None, N_MOD_ROWS, tn), lambda l, j: (l, 0, j)),
        out_shape=jax.ShapeDtypeStruct((depth, N_MOD_ROWS, n), F32),
        compiler_params=_params("parallel", "parallel"),
        name="modulation",
    )(cvec, w_mod, b_mod.reshape(depth, 1, n))
    return out.reshape(depth, N_MOD_ROWS, 6, 1, d)


def _head_rms(a, g, mul):
    outs = []
    for h in range(N_HEADS):
        t = a[:, h * HEAD_DIM:(h + 1) * HEAD_DIM]
        ms = jnp.mean(t * t, axis=-1, keepdims=True)
        y = (t * lax.rsqrt(ms + EPS)) * g
        outs.append(y if mul is None else y * mul)
    return jnp.concatenate(outs, axis=-1)


def _even_proj_kernel(x_ref, g_ref, sh_ref, sc_ref, w_ref, qg_ref, kg_ref, lng_ref, lnb_ref, *rest, with_kv):
    if with_kv:
        p_ref, kv_ref, h_ref, acc_ref = rest
    else:
        p_ref, h_ref, acc_ref = rest
        kv_ref = None
    j = pl.program_id(1)

    @pl.when(j == 0)
    def _():
        _fill_modulated(x_ref, g_ref, sh_ref, sc_ref, h_ref)

    acc_ref[...] = jnp.dot(h_ref[...], w_ref[...], preferred_element_type=F32)
    n_chunks = acc_ref.shape[0] // ROW_CHUNK

    def chunked(fn):
        def body(c, carry):
            r0 = pl.multiple_of(c * ROW_CHUNK, ROW_CHUNK)
            fn(pl.ds(r0, ROW_CHUNK), acc_ref[pl.ds(r0, ROW_CHUNK), :])
            return carry
        lax.fori_loop(0, n_chunks, body, 0)

    @pl.when(j == 0)
    def _():
        qg = qg_ref[...]

        def fn(rows, a):
            p_ref[rows, :] = _head_rms(a, qg, ATTN_SCALE).astype(BF16)
        chunked(fn)

    @pl.when(j == 1)
    def _():
        kg = kg_ref[...]

        def fn(rows, a):
            k = _head_rms(a, kg, None)
            p_ref[rows, :] = k.astype(BF16)
            if with_kv:
                kv_ref[rows, :] = k
        chunked(fn)

    @pl.when(j == 2)
    def _():
        def fn(rows, a):
            p_ref[rows, :] = a.astype(BF16)
            if with_kv:
                kv_ref[rows, :] = a
        chunked(fn)

    @pl.when(j == 3)
    def _():
        def fn(rows, a):
            p_ref[rows, :] = jax.nn.gelu(a).astype(BF16)
        chunked(fn)

    @pl.when(j == 4)
    def _():
        lng, lnb = lng_ref[...], lnb_ref[...]

        def fn(rows, a):
            t = jax.nn.gelu(a)
            mu = jnp.mean(t, axis=-1, keepdims=True)
            var = jnp.mean(jnp.square(t - mu), axis=-1, keepdims=True)
            y = (t - mu) * lax.rsqrt(var + EPS)
            p_ref[rows, :] = (y * lng + lnb).astype(BF16)
        chunked(fn)


def _even_projection(x, mods, layer, row_of, g1, w_in, qg, kg, lng, lnb, with_kv):
    m, d = x.shape
    n = w_in.shape[1]
    out_shape = [jax.ShapeDtypeStruct((m, n), BF16)]
    out_specs = [pl.BlockSpec((TM, TN), lambda i, j: (i, j))]
    if with_kv:
        out_shape.append(jax.ShapeDtypeStruct((m, 2 * D_A), F32))
        out_specs.append(pl.BlockSpec((TM, TN), lambda i, j: (i, jnp.clip(j - 1, 0, 1))))
    vec = lambda w: pl.BlockSpec((1, w), lambda i, j: (0, 0))
    res = pl.pallas_call(
        functools.partial(_even_proj_kernel, with_kv=with_kv),
        grid=(m // TM, n // TN),
        in_specs=[
            pl.BlockSpec((TM, d), lambda i, j: (i, 0)),
            vec(d),
            _mod_spec(layer, 0, row_of),
            _mod_spec(layer, 1, row_of),
            pl.BlockSpec((d, TN), lambda i, j: (0, j)),
            vec(HEAD_DIM), vec(HEAD_DIM), vec(D_B), vec(D_B),
        ],
        out_specs=out_specs,
        out_shape=out_shape,
        scratch_shapes=[pltpu.VMEM((TM, d), BF16), pltpu.VMEM((TM, TN), F32)],
        compiler_params=_params("parallel", "arbitrary"),
        name="even_projection",
    )(x, g1, mods, mods, w_in, qg, kg, lng, lnb)
    return res if with_kv else (res[0], None)


_NT = (((1,), (1,)), ((), ()))


def _ctx_attn_kernel(q_ref, k_ref, v_ref, o_ref):
    for h in range(N_HEADS):
        cols = slice(h * HEAD_DIM, (h + 1) * HEAD_DIM)
        s = lax.dot_general(q_ref[:, cols], k_ref[:, cols], _NT, preferred_element_type=F32)
        m = jnp.max(s, axis=-1, keepdims=True)
        p = jnp.exp(s - m)
        l = jnp.sum(p, axis=-1, keepdims=True)
        o = jnp.dot(p.astype(BF16), v_ref[:, cols], preferred_element_type=F32)
        o_ref[:, cols] = (o / l).astype(o_ref.dtype)


def _context_attention(p, batch, seq):
    return pl.pallas_call(
        _ctx_attn_kernel,
        grid=(batch,),
        in_specs=[
            pl.BlockSpec((seq, D_A), lambda b: (b, 0)),
            pl.BlockSpec((seq, D_A), lambda b: (b, 1)),
            pl.BlockSpec((seq, D_A), lambda b: (b, 2)),
        ],
        out_specs=pl.BlockSpec((seq, D_A), lambda b: (b, 0)),
        out_shape=jax.ShapeDtypeStruct((batch * seq, D_A), BF16),
        compiler_params=_params("parallel"),
        name="context_attention",
    )(p, p, p)


def _bias_kernel(rpb_ref, o_ref):
    v = pl.program_id(0)
    q = lax.broadcasted_iota(I32, (GRID_W, GRID_W), 0)
    k = lax.broadcasted_iota(I32, (GRID_W, GRID_W), 1)
    dc = jnp.clip(k - q, -(WIN_COLS - 1), WIN_COLS - 1) + (WIN_COLS - 1)
    col_start = jnp.clip(q - WIN_COLS // 2, 0, GRID_W - WIN_COLS)
    col_in = (k >= col_start) & (k < col_start + WIN_COLS)
    n_dr, n_dc = 2 * WIN_ROWS - 1, 2 * WIN_COLS - 1
    for h in range(N_HEADS):
        for i in range(WIN_ROWS):
            base = (h * n_dr + v + i) * n_dc
            tile = jnp.full((GRID_W, GRID_W), NEG_INF, F32)
            for d in range(n_dc):
                tile = jnp.where(dc == d, rpb_ref[base + d], tile)
            o_ref[h, i] = jnp.where(col_in, tile, NEG_INF)


def _bias_table(rpb_e):
    tab = pl.pallas_call(
        _bias_kernel,
        grid=(WIN_ROWS,),
        in_specs=[pl.BlockSpec(memory_space=pltpu.SMEM)],
        out_specs=pl.BlockSpec((None, N_HEADS, WIN_ROWS, GRID_W, GRID_W), lambda v: (v, 0, 0, 0, 0)),
        out_shape=jax.ShapeDtypeStruct((WIN_ROWS, N_HEADS, WIN_ROWS, GRID_W, GRID_W), F32),
        compiler_params=_params("parallel"),
        name="rpb_table",
    )(rpb_e.reshape(-1))
    return tab.transpose(0, 1, 3, 2, 4).reshape(WIN_ROWS, N_HEADS, GRID_W, WIN_ROWS * GRID_W)


def _nbr_attn_kernel(q_ref, k_ref, v_ref, ck_ref, cv_ref, b_ref, o_ref, *, rows):
    r = pl.program_id(1)
    row_start = jnp.clip(r - WIN_ROWS // 2, 0, rows - WIN_ROWS)
    k0 = pl.multiple_of(row_start * GRID_W, GRID_W)
    n_loc = WIN_ROWS * GRID_W
    for h in range(N_HEADS):
        cols = slice(h * HEAD_DIM, (h + 1) * HEAD_DIM)
        q = q_ref[:, cols]
        s_loc = lax.dot_general(q, k_ref[pl.ds(k0, n_loc), cols], _NT, preferred_element_type=F32) + b_ref[h]
        s_ctx = lax.dot_general(q, ck_ref[:, cols], _NT, preferred_element_type=F32)
        m = jnp.maximum(jnp.max(s_loc, axis=-1, keepdims=True), jnp.max(s_ctx, axis=-1, keepdims=True))
        p_loc = jnp.exp(s_loc - m)
        p_ctx = jnp.exp(s_ctx - m)
        l = jnp.sum(p_loc, axis=-1, keepdims=True) + jnp.sum(p_ctx, axis=-1, keepdims=True)
        o = (jnp.dot(p_loc.astype(BF16), v_ref[pl.ds(k0, n_loc), cols], preferred_element_type=F32)
             + jnp.dot(p_ctx.astype(BF16), cv_ref[:, cols], preferred_element_type=F32))
        o_ref[:, cols] = (o / l).astype(o_ref.dtype)


def _neighbourhood_attention(p, ck, cv, bias_tab, batch, seq):
    rows = seq // GRID_W
    past = ck.shape[1]

    def placement(r):
        return jnp.clip(r - WIN_ROWS // 2, 0, rows - WIN_ROWS) - r + (WIN_ROWS - 1)

    return pl.pallas_call(
        functools.partial(_nbr_attn_kernel, rows=rows),
        grid=(batch, rows),
        in_specs=[
            pl.BlockSpec((GRID_W, D_A), lambda b, r: (b * rows + r, 0)),
            pl.BlockSpec((seq, D_A), lambda b, r: (b, 1)),
            pl.BlockSpec((seq, D_A), lambda b, r: (b, 2)),
            pl.BlockSpec((None, past, D_A), lambda b, r: (b, 0, 0)),
            pl.BlockSpec((None, past, D_A), lambda b, r: (b, 0, 0)),
            pl.BlockSpec((None, N_HEADS, GRID_W, WIN_ROWS * GRID_W), lambda b, r: (placement(r), 0, 0, 0)),
        ],
        out_specs=pl.BlockSpec((GRID_W, D_A), lambda b, r: (b * rows + r, 0)),
        out_shape=jax.ShapeDtypeStruct((batch * seq, D_A), BF16),
        compiler_params=_params("parallel", "arbitrary"),
        name="neighbourhood_attention",
    )(p, p, p, ck, cv, bias_tab)


def _sgu_kernel(u_ref, v_ref, w_ref, bt_ref, o_ref):
    for n in range(u_ref.shape[0] // CHUNK):
        rows = slice(n * CHUNK, (n + 1) * CHUNK)
        for g in range(N_GROUPS):
            cols = slice(g * GROUP, (g + 1) * GROUP)
            s = jnp.dot(w_ref[g], v_ref[rows, cols], preferred_element_type=F32) + bt_ref[:, g:g + 1]
            o_ref[rows, cols] = (u_ref[rows, cols].astype(F32) * s).astype(o_ref.dtype)


def _spatial_gating(p, w_s, b_s_t):
    m = p.shape[0]
    tq = 4 * CHUNK
    return pl.pallas_call(
        _sgu_kernel,
        grid=(m // tq,),
        in_specs=[
            pl.BlockSpec((tq, D_B), lambda i: (i, 3)),
            pl.BlockSpec((tq, D_B), lambda i: (i, 4)),
            pl.BlockSpec((N_GROUPS, CHUNK, CHUNK), lambda i: (0, 0, 0)),
            pl.BlockSpec((CHUNK, N_GROUPS), lambda i: (0, 0)),
        ],
        out_specs=pl.BlockSpec((tq, D_B), lambda i: (i, 0)),
        out_shape=jax.ShapeDtypeStruct((m, D_B), BF16),
        compiler_params=_params("parallel"),
        name="spatial_gating",
    )(p, p, w_s, b_s_t)


def _out_proj_kernel(x_ref, a_ref, s_ref, w_ref, gate_ref, o_ref):
    acc = jnp.dot(a_ref[...], w_ref[:D_A, :], preferred_element_type=F32)
    acc = acc + jnp.dot(s_ref[...], w_ref[D_A:, :], preferred_element_type=F32)
    o_ref[...] = x_ref[...] + gate_ref[...] * acc


def _out_projection(x, a, s, w_out, mods, layer, row_of):
    m, d = x.shape
    return pl.pallas_call(
        _out_proj_kernel,
        grid=(m // TM, d // TN),
        in_specs=[
            pl.BlockSpec((TM, TN), lambda i, j: (i, j)),
            pl.BlockSpec((TM, D_A), lambda i, j: (i, 0)),
            pl.BlockSpec((TM, D_B), lambda i, j: (i, 0)),
            pl.BlockSpec((D_A + D_B, TN), lambda i, j: (0, j)),
            _mod_spec(layer, 2, row_of, TN, lambda j: j),
        ],
        out_specs=pl.BlockSpec((TM, TN), lambda i, j: (i, j)),
        out_shape=jax.ShapeDtypeStruct((m, d), F32),
        input_output_aliases={0: 0},
        compiler_params=_params("parallel", "parallel"),
        name="even_out_projection",
    )(x, a, s, w_out, mods)


def _ffn_kernel(x_ref, g_ref, sh_ref, sc_ref, wg_ref, wu_ref, wd_ref, gate_ref, o_ref, h_ref, acc_ref):
    f = pl.program_id(1)

    @pl.when(f == 0)
    def _():
        _fill_modulated(x_ref, g_ref, sh_ref, sc_ref, h_ref)

    h = h_ref[...]
    a = jnp.dot(h, wg_ref[...], preferred_element_type=F32)
    u = jnp.dot(h, wu_ref[...], preferred_element_type=F32)
    y = jnp.dot((jax.nn.silu(a) * u).astype(BF16), wd_ref[...], preferred_element_type=F32)

    @pl.when(f == 0)
    def _():
        acc_ref[...] = y

    @pl.when(f > 0)
    def _():
        acc_ref[...] += y

    @pl.when(f == pl.num_programs(1) - 1)
    def _():
        o_ref[...] = x_ref[...] + gate_ref[...] * acc_ref[...]


def _dense_ffn(x, mods, layer, row_of, g2, wg, wu, wd):
    m, d = x.shape
    ff = wg.shape[1]
    row_tile = lambda i: row_of(i)
    return pl.pallas_call(
        _ffn_kernel,
        grid=(m // TM_FFN, ff // TF_FFN),
        in_specs=[
            pl.BlockSpec((TM_FFN, d), lambda i, f: (i, 0)),
            pl.BlockSpec((1, d), lambda i, f: (0, 0)),
            _mod_spec(layer, 3, row_tile),
            _mod_spec(layer, 4, row_tile),
            pl.BlockSpec((d, TF_FFN), lambda i, f: (0, f)),
            pl.BlockSpec((d, TF_FFN), lambda i, f: (0, f)),
            pl.BlockSpec((TF_FFN, d), lambda i, f: (f, 0)),
            _mod_spec(layer, 5, row_tile),
        ],
        out_specs=pl.BlockSpec((TM_FFN, d), lambda i, f: (i, 0)),
        out_shape=jax.ShapeDtypeStruct((m, d), F32),
        scratch_shapes=[pltpu.VMEM((TM_FFN, d), BF16), pltpu.VMEM((TM_FFN, d), F32)],
        input_output_aliases={0: 0},
        compiler_params=_params("parallel", "arbitrary"),
        name="dense_swiglu",
    )(x, g2, mods, mods, wg, wu, wd, mods)


def _glu_kernel(x_ref, g_ref, sh_ref, sc_ref, wa_ref, wg_ref, ba_ref, bg_ref, z_ref, h_ref):
    @pl.when(pl.program_id(1) == 0)
    def _():
        _fill_modulated(x_ref, g_ref, sh_ref, sc_ref, h_ref)

    h = h_ref[...]
    a = jnp.dot(h, wa_ref[...], preferred_element_type=F32) + ba_ref[...]
    gt = jnp.dot(h, wg_ref[...], preferred_element_type=F32) + bg_ref[...]
    z_ref[...] = a * jax.nn.sigmoid(gt)


def _conv_glu(x, mods, layer, row_of, g1, w1, b1):
    m, d = x.shape
    nj = d // TN
    return pl.pallas_call(
        _glu_kernel,
        grid=(m // TM, nj),
        in_specs=[
            pl.BlockSpec((TM, d), lambda i, j: (i, 0)),
            pl.BlockSpec((1, d), lambda i, j: (0, 0)),
            _mod_spec(layer, 0, row_of),
            _mod_spec(layer, 1, row_of),
            pl.BlockSpec((d, TN), lambda i, j: (0, j)),
            pl.BlockSpec((d, TN), lambda i, j: (0, j + nj)),
            pl.BlockSpec((1, TN), lambda i, j: (0, j)),
            pl.BlockSpec((1, TN), lambda i, j: (0, j + nj)),
        ],
        out_specs=pl.BlockSpec((TM, TN), lambda i, j: (i, j)),
        out_shape=jax.ShapeDtypeStruct((m, d), F32),
        scratch_shapes=[pltpu.VMEM((TM, d), BF16)],
        compiler_params=_params("parallel", "arbitrary"),
        name="conv_glu",
    )(x, g1, mods, mods, w1, w1, b1, b1)


def _dwconv_kernel(z_ref, w_ref, b_ref, o_ref, pad_ref):
    seq, tc = z_ref.shape
    rc = 64
    zeros = jnp.zeros((CONV_PAD, tc), F32)
    pad_ref[0:CONV_PAD, :] = zeros
    pad_ref[CONV_PAD + seq:, :] = zeros

    def copy(c, carry):
        r0 = pl.multiple_of(c * rc, rc)
        pad_ref[pl.ds(CONV_PAD + r0, rc), :] = z_ref[pl.ds(r0, rc), :]
        return carry

    lax.fori_loop(0, seq // rc, copy, 0)
    bias = jnp.broadcast_to(b_ref[...], (rc, tc))
    first = CONV_PAD - CONV_W // 2
    win_rows = rc + 2 * CONV_PAD
    sub = 8

    def conv(c, carry):
        r0 = pl.multiple_of(c * rc, rc)
        win = pad_ref[pl.ds(r0, win_rows), :]
        acc = bias
        for s in range(sub):
            shifted = win if s == 0 else pltpu.roll(win, win_rows - s, 0)
            for k in range(CONV_W):
                off = first + k
                if off % sub == s:
                    a0 = off - s
                    acc = acc + shifted[a0:a0 + rc, :] * w_ref[k:k + 1, :]
        o_ref[pl.ds(r0, rc), :] = acc
        return carry

    lax.fori_loop(0, seq // rc, conv, 0)


def _depthwise_conv(z, batch, seq, dw, dw_b):
    d = z.shape[1]
    tc = 256
    out = pl.pallas_call(
        _dwconv_kernel,
        grid=(batch, d // tc),
        in_specs=[
            pl.BlockSpec((None, seq, tc), lambda b, c: (b, 0, c)),
            pl.BlockSpec((CONV_W, tc), lambda b, c: (0, c)),
            pl.BlockSpec((1, tc), lambda b, c: (0, c)),
        ],
        out_specs=pl.BlockSpec((None, seq, tc), lambda b, c: (b, 0, c)),
        out_shape=jax.ShapeDtypeStruct((batch, seq, d), F32),
        scratch_shapes=[pltpu.VMEM((seq + 2 * CONV_PAD, tc), F32)],
        compiler_params=_params("parallel", "parallel"),
        name="depthwise_conv",
    )(z.reshape(batch, seq, d), dw, dw_b)
    return out.reshape(batch * seq, d)


def _conv_out_kernel(z_ref, lng_ref, lnb_ref, w_ref, b_ref, x_ref, gate_ref, o_ref, h_ref):
    @pl.when(pl.program_id(1) == 0)
    def _():
        lng, lnb = lng_ref[...], lnb_ref[...]

        def body(c, carry):
            r0 = pl.multiple_of(c * ROW_CHUNK, ROW_CHUNK)
            z = z_ref[pl.ds(r0, ROW_CHUNK), :]
            mu = jnp.mean(z, axis=-1, keepdims=True)
            var = jnp.mean(jnp.square(z - mu), axis=-1, keepdims=True)
            y = ((z - mu) * lax.rsqrt(var + EPS)) * lng + lnb
            h_ref[pl.ds(r0, ROW_CHUNK), :] = jax.nn.silu(y).astype(BF16)
            return carry

        lax.fori_loop(0, z_ref.shape[0] // ROW_CHUNK, body, 0)

    acc = jnp.dot(h_ref[...], w_ref[...], preferred_element_type=F32) + b_ref[...]
    o_ref[...] = x_ref[...] + gate_ref[...] * acc


def _conv_out(zc, x, mods, layer, row_of, ln_g, ln_b, w2, b2):
    m, d = x.shape
    return pl.pallas_call(
        _conv_out_kernel,
        grid=(m // TM, d // TN),
        in_specs=[
            pl.BlockSpec((TM, d), lambda i, j: (i, 0)),
            pl.BlockSpec((1, d), lambda i, j: (0, 0)),
            pl.BlockSpec((1, d), lambda i, j: (0, 0)),
            pl.BlockSpec((d, TN), lambda i, j: (0, j)),
            pl.BlockSpec((1, TN), lambda i, j: (0, j)),
            pl.BlockSpec((TM, TN), lambda i, j: (i, j)),
            _mod_spec(layer, 2, row_of, TN, lambda j: j),
        ],
        out_specs=pl.BlockSpec((TM, TN), lambda i, j: (i, j)),
        out_shape=jax.ShapeDtypeStruct((m, d), F32),
        scratch_shapes=[pltpu.VMEM((TM, d), BF16)],
        input_output_aliases={5: 0},
        compiler_params=_params("parallel", "arbitrary"),
        name="conv_out_projection",
    )(zc, ln_g, ln_b, w2, b2, x, mods)


def _router_kernel(x_ref, g_ref, sh_ref, sc_ref, rw_ref, base_ref, h_ref, ri_ref, rg_ref, cnt_ref, run_ref, tri_ref):
    tm = x_ref.shape[0]

    @pl.when(pl.program_id(0) == 0)
    def _():
        run_ref[...] = base_ref[...]
        src = lax.broadcasted_iota(I32, (tm, tm), 0)
        dst = lax.broadcasted_iota(I32, (tm, tm), 1)
        tri_ref[...] = jnp.where(src < dst, 1.0, 0.0).astype(BF16)

    _fill_modulated(x_ref, g_ref, sh_ref, sc_ref, h_ref)
    logits = lax.dot_general(rw_ref[...], h_ref[...], _NT, precision=lax.Precision.HIGHEST,
                             preferred_element_type=F32)
    eid = lax.broadcasted_iota(I32, logits.shape, 0).astype(F32)
    none = float(N_EXPERTS)
    m1 = jnp.max(logits, axis=0, keepdims=True)
    i1 = jnp.min(jnp.where(logits == m1, eid, none), axis=0, keepdims=True)
    rest = jnp.where(eid == i1, -jnp.inf, logits)
    m2 = jnp.max(rest, axis=0, keepdims=True)
    i2 = jnp.min(jnp.where(rest == m2, eid, none), axis=0, keepdims=True)
    e2 = jnp.exp(m2 - m1)
    gate1 = 1.0 / (1.0 + e2)
    gate2 = e2 / (1.0 + e2)
    hot1 = eid == i1
    hot2 = eid == i2
    chosen = jnp.where(hot1, 1.0, jnp.where(hot2, 1.0, 0.0))
    before = jnp.dot(chosen.astype(BF16), tri_ref[...], preferred_element_type=F32) + run_ref[:, 0:1]
    rank1 = jnp.sum(jnp.where(hot1, before, 0.0), axis=0, keepdims=True)
    rank2 = jnp.sum(jnp.where(hot2, before, 0.0), axis=0, keepdims=True)
    ri_ref[...] = jnp.zeros(ri_ref.shape, I32)
    ri_ref[0:1, :] = i1.astype(I32)
    ri_ref[1:2, :] = i2.astype(I32)
    ri_ref[2:3, :] = rank1.astype(I32)
    ri_ref[3:4, :] = rank2.astype(I32)
    rg_ref[...] = jnp.zeros(rg_ref.shape, F32)
    rg_ref[0:1, :] = gate1
    rg_ref[1:2, :] = gate2
    run_ref[...] = run_ref[...] + jnp.sum(chosen, axis=1, keepdims=True)
    cnt_ref[...] = run_ref[...]


def _router(x, mods, layer, row_of, g2, rw_t, base):
    m, d = x.shape
    tm = TM_ROUTE
    return pl.pallas_call(
        _router_kernel,
        grid=(m // tm,),
        in_specs=[
            pl.BlockSpec((tm, d), lambda i: (i, 0)),
            pl.BlockSpec((1, d), lambda i: (0, 0)),
            _mod_spec(layer, 3, row_of),
            _mod_spec(layer, 4, row_of),
            pl.BlockSpec((N_EXPERTS, d), lambda i: (0, 0)),
            pl.BlockSpec((N_EXPERTS, LANE), lambda i: (0, 0)),
        ],
        out_specs=[
            pl.BlockSpec((tm, d), lambda i: (i, 0)),
            pl.BlockSpec((8, tm), lambda i: (0, i)),
            pl.BlockSpec((8, tm), lambda i: (0, i)),
            pl.BlockSpec((N_EXPERTS, LANE), lambda i: (0, 0)),
        ],
        out_shape=[
            jax.ShapeDtypeStruct((m, d), F32),
            jax.ShapeDtypeStruct((8, m), I32),
            jax.ShapeDtypeStruct((8, m), F32),
            jax.ShapeDtypeStruct((N_EXPERTS, LANE), F32),
        ],
        scratch_shapes=[pltpu.VMEM((N_EXPERTS, LANE), F32), pltpu.VMEM((tm, tm), BF16)],
        compiler_params=_params("arbitrary"),
        name="moe_router",
    )(x, g2, mods, mods, rw_t, base)


def _dispatch_kernel(pos_ref, h_ref, xs_in_ref, xs_ref, sem):
    del xs_in_ref
    tm = h_ref.shape[0]
    n_tok = pos_ref.shape[0] // 2
    t0 = pl.program_id(0) * tm

    def row_copy(t, k):
        dst = pos_ref[k * n_tok + t0 + t]
        return pltpu.make_async_copy(h_ref.at[pl.ds(t, 1)], xs_ref.at[pl.ds(dst, 1)], sem)

    def start(t, carry):
        row_copy(t, 0).start()
        row_copy(t, 1).start()
        return carry

    def wait(t, carry):
        row_copy(t, 0).wait()
        row_copy(t, 1).wait()
        return carry

    lax.fori_loop(0, tm, start, 0)
    lax.fori_loop(0, tm, wait, 0)


def _dispatch(pos_flat, h, xs):
    m, d = h.shape
    return pl.pallas_call(
        _dispatch_kernel,
        grid_spec=pltpu.PrefetchScalarGridSpec(
            num_scalar_prefetch=1,
            grid=(m // TM_MOVE,),
            in_specs=[
                pl.BlockSpec((TM_MOVE, d), lambda i, pos: (i, 0)),
                pl.BlockSpec(memory_space=pl.ANY),
            ],
            out_specs=pl.BlockSpec(memory_space=pl.ANY),
            scratch_shapes=[pltpu.SemaphoreType.DMA(())],
        ),
        out_shape=jax.ShapeDtypeStruct(xs.shape, xs.dtype),
        input_output_aliases={2: 0},
        compiler_params=_params("arbitrary"),
        name="moe_dispatch",
    )(pos_flat, h, xs)


def _expert_kernel(te_ref, nu_ref, xs_ref, wg_ref, wu_ref, wd_ref, ys_ref, xb_ref, acc_ref):
    t = pl.program_id(0)
    f = pl.program_id(1)
    last = pl.num_programs(1) - 1
    used = t < nu_ref[0]

    @pl.when(used & (f == 0))
    def _():
        xb_ref[...] = xs_ref[...].astype(BF16)

    @pl.when(used)
    def _():
        h = xb_ref[...]
        a = jnp.dot(h, wg_ref[...], preferred_element_type=F32)
        u = jnp.dot(h, wu_ref[...], preferred_element_type=F32)
        y = jnp.dot((jax.nn.silu(a) * u).astype(BF16), wd_ref[...], preferred_element_type=F32)

        @pl.when(f == 0)
        def _():
            acc_ref[...] = y

        @pl.when(f > 0)
        def _():
            acc_ref[...] += y

    @pl.when(used & (f == last))
    def _():
        ys_ref[...] = acc_ref[...]

    @pl.when(jnp.logical_not(used) & (f == last))
    def _():
        ys_ref[...] = jnp.zeros(ys_ref.shape, ys_ref.dtype)


def _experts(tile_expert, n_used, xs, wg, wu, wd):
    p, d = xs.shape
    ff = wg.shape[2]
    nf = ff // TF_MOE

    def row_idx(t, f, te, nu):
        return (jnp.minimum(t, nu[0] - 1), 0)

    def hid(t, f, nu):
        return jnp.where(t < nu[0], f, nf - 1)

    return pl.pallas_call(
        _expert_kernel,
        grid_spec=pltpu.PrefetchScalarGridSpec(
            num_scalar_prefetch=2,
            grid=(p // TM_FFN, nf),
            in_specs=[
                pl.BlockSpec((TM_FFN, d), row_idx),
                pl.BlockSpec((None, d, TF_MOE), lambda t, f, te, nu: (te[t], 0, hid(t, f, nu))),
                pl.BlockSpec((None, d, TF_MOE), lambda t, f, te, nu: (te[t], 0, hid(t, f, nu))),
                pl.BlockSpec((None, TF_MOE, d), lambda t, f, te, nu: (te[t], hid(t, f, nu), 0)),
            ],
            out_specs=pl.BlockSpec((TM_FFN, d), lambda t, f, te, nu: (t, 0)),
            scratch_shapes=[pltpu.VMEM((TM_FFN, d), BF16), pltpu.VMEM((TM_FFN, d), F32)],
        ),
        out_shape=jax.ShapeDtypeStruct((p, d), F32),
        compiler_params=_params("arbitrary", "arbitrary"),
        name="moe_experts",
    )(tile_expert, n_used, xs, wg, wu, wd)


def _combine_kernel(pos_ref, ys_ref, x_ref, gt_ref, gate_ref, o_ref, buf_ref, sem):
    tm = x_ref.shape[0]
    n_tok = pos_ref.shape[0] // 2
    t0 = pl.program_id(0) * tm

    def row_copy(t, k):
        src = pos_ref[k * n_tok + t0 + t]
        return pltpu.make_async_copy(ys_ref.at[pl.ds(src, 1)], buf_ref.at[k, pl.ds(t, 1)], sem)

    def start(t, carry):
        row_copy(t, 0).start()
        row_copy(t, 1).start()
        return carry

    def wait(t, carry):
        row_copy(t, 0).wait()
        row_copy(t, 1).wait()
        return carry

    lax.fori_loop(0, tm, start, 0)
    lax.fori_loop(0, tm, wait, 0)
    gate = gate_ref[...]

    def mix(c, carry):
        r0 = pl.multiple_of(c * ROW_CHUNK, ROW_CHUNK)
        rows = pl.ds(r0, ROW_CHUNK)
        g = gt_ref[rows, :]
        y = g[:, 0:1] * buf_ref[0, rows, :] + g[:, 1:2] * buf_ref[1, rows, :]
        o_ref[rows, :] = x_ref[rows, :] + gate * y
        return carry

    lax.fori_loop(0, tm // ROW_CHUNK, mix, 0)


def _combine(pos_flat, ys, x, gates_t, mods, layer, row_of):
    m, d = x.shape
    return pl.pallas_call(
        _combine_kernel,
        grid_spec=pltpu.PrefetchScalarGridSpec(
            num_scalar_prefetch=1,
            grid=(m // TM_MOVE,),
            in_specs=[
                pl.BlockSpec(memory_space=pl.ANY),
                pl.BlockSpec((TM_MOVE, d), lambda i, pos: (i, 0)),
                pl.BlockSpec((TM_MOVE, 2), lambda i, pos: (i, 0)),
                _mod_spec(layer, 5, row_of),
            ],
            out_specs=pl.BlockSpec((TM_MOVE, d), lambda i, pos: (i, 0)),
            scratch_shapes=[pltpu.VMEM((2, TM_MOVE, d), F32), pltpu.SemaphoreType.DMA(())],
        ),
        out_shape=jax.ShapeDtypeStruct((m, d), F32),
        input_output_aliases={2: 0},
        compiler_params=_params("arbitrary"),
        name="moe_combine",
    )(pos_flat, ys, x, gates_t, mods)


def _moe(streams, mods, layer, g2, rw_t, wg, wu, wd):
    n_rows = sum(2 * x.shape[0] for x, _, _ in streams)
    n_tiles = n_rows // TM_FFN + N_EXPERTS
    base = jnp.zeros((N_EXPERTS, LANE), F32)
    routed = []
    for x, row_of, tile in streams:
        h, ri, rg, base = _router(x, mods, layer, lambda i, ro=row_of: ro(i, TM_ROUTE), g2, rw_t, base)
        routed.append((h, ri, rg))
    counts = base[:, 0].astype(I32)
    tiles_e = (counts + TM_FFN - 1) // TM_FFN
    tile_end = jnp.cumsum(tiles_e)
    row_off = (tile_end - tiles_e) * TM_FFN
    n_used = tile_end[-1:]
    tile_ids = jnp.minimum(jnp.arange(n_tiles, dtype=I32), n_used[0] - 1)
    tile_expert = jnp.sum(tile_ids[:, None] >= tile_end[None, :], axis=1).astype(I32)
    xs = jnp.zeros((n_tiles * TM_FFN, D_MODEL), F32)
    pos_all = []
    for h, ri, rg in routed:
        ids, rank = ri[0:2], ri[2:4]
        off = jnp.sum(jnp.where(ids[:, :, None] == jnp.arange(N_EXPERTS)[None, None, :], row_off[None, None, :], 0), axis=-1)
        pos = (off + rank).reshape(-1)
        pos_all.append(pos)
        xs = _dispatch(pos, h, xs)
    ys = _experts(tile_expert, n_used, xs, wg, wu, wd)
    outs = []
    for (x, row_of, tile), (h, ri, rg), pos in zip(streams, routed, pos_all):
        outs.append(_combine(pos, ys, x, rg[0:2].T, mods, layer, lambda i, ro=row_of: ro(i, TM_MOVE)))
    return outs


def kernel(x_prompt, x_sample, cache_k, cache_v, c, c_ctx, w_mod, b_mod, norm1_g, norm2_g, w_in_even, q_norm_g, k_norm_g, rpb, sgu_ln_g, sgu_ln_b, sgu_w, sgu_b, w_out_even, ffn_w_gate, ffn_w_up, ffn_w_down, conv_w1, conv_b1, conv_dw, conv_dw_b, conv_ln_g, conv_ln_b, conv_w2, conv_b2, router_w, moe_w_gate, moe_w_up, moe_w_down):
    batch, seq, d = x_prompt.shape
    dec_batch, dec_seq, _ = x_sample.shape
    depth = w_mod.shape[0]
    n_even = w_in_even.shape[0]
    past = cache_k.shape[2]
    xp = x_prompt.reshape(batch * seq, d)
    xs = x_sample.reshape(dec_batch * dec_seq, d)

    cvec = jnp.zeros((N_MOD_ROWS, d), F32).at[0].set(c_ctx).at[1:1 + dec_batch].set(c)
    mods = _modulation(cvec, w_mod, b_mod)

    prompt_row = lambda i, tile=TM: 0
    sample_row = lambda i, tile=TM: 1 + (i * tile) // dec_seq
    row2 = lambda a: a.reshape(1, -1)

    new_k, new_v = [], []
    for l in range(depth):
        g1, g2 = row2(norm1_g[l]), row2(norm2_g[l])
        if l % 2 == 0:
            e = l // 2
            w_in = w_in_even[e].astype(BF16)
            w_out = w_out_even[e].astype(BF16)
            w_s = sgu_w[e].astype(BF16)
            b_s_t = sgu_b[e].T
            ff_pad = D_FF_PAD - ffn_w_gate.shape[2]
            wg = jnp.pad(ffn_w_gate[e].astype(BF16), ((0, 0), (0, ff_pad)))
            wu = jnp.pad(ffn_w_up[e].astype(BF16), ((0, 0), (0, ff_pad)))
            wd = jnp.pad(ffn_w_down[e].astype(BF16), ((0, ff_pad), (0, 0)))
            ck = cache_k[:, e].reshape(dec_batch, past, D_A).astype(BF16)
            cv = cache_v[:, e].reshape(dec_batch, past, D_A).astype(BF16)
            bias_tab = _bias_table(rpb[e])
            norm_args = (row2(q_norm_g[e]), row2(k_norm_g[e]), row2(sgu_ln_g[e]), row2(sgu_ln_b[e]))

            pp, kv = _even_projection(xp, mods, l, prompt_row, g1, w_in, *norm_args, with_kv=True)
            new_k.append(kv[:, :D_A].reshape(batch, seq, N_HEADS, HEAD_DIM))
            new_v.append(kv[:, D_A:].reshape(batch, seq, N_HEADS, HEAD_DIM))
            ap = _context_attention(pp, batch, seq)
            sp = _spatial_gating(pp, w_s, b_s_t)
            xp = _out_projection(xp, ap, sp, w_out, mods, l, prompt_row)

            ps, _ = _even_projection(xs, mods, l, sample_row, g1, w_in, *norm_args, with_kv=False)
            a_s = _neighbourhood_attention(ps, ck, cv, bias_tab, dec_batch, dec_seq)
            s_s = _spatial_gating(ps, w_s, b_s_t)
            xs = _out_projection(xs, a_s, s_s, w_out, mods, l, sample_row)

            xp = _dense_ffn(xp, mods, l, lambda i: prompt_row(i, TM_FFN), g2, wg, wu, wd)
            xs = _dense_ffn(xs, mods, l, lambda i: sample_row(i, TM_FFN), g2, wg, wu, wd)
        else:
            o = l // 2
            w1 = conv_w1[o].astype(BF16)
            w2 = conv_w2[o].astype(BF16)
            b1, b2 = row2(conv_b1[o]), row2(conv_b2[o])
            dw, dw_b = conv_dw[o], row2(conv_dw_b[o])
            ln_g, ln_b = row2(conv_ln_g[o]), row2(conv_ln_b[o])
            outs = []
            for x, row_of, nb, ns in ((xp, prompt_row, batch, seq), (xs, sample_row, dec_batch, dec_seq)):
                z = _conv_glu(x, mods, l, row_of, g1, w1, b1)
                zc = _depthwise_conv(z, nb, ns, dw, dw_b)
                outs.append(_conv_out(zc, x, mods, l, row_of, ln_g, ln_b, w2, b2))
            xp, xs = outs
            xp, xs = _moe(
                [(xp, prompt_row, None), (xs, sample_row, None)], mods, l, g2,
                router_w[o].T, moe_w_gate[o].astype(BF16), moe_w_up[o].astype(BF16), moe_w_down[o].astype(BF16))

    new_cache_k = jnp.stack(new_k, axis=1)
    new_cache_v = jnp.stack(new_v, axis=1)
    return (xp.reshape(batch, seq, d), xs.reshape(dec_batch, dec_seq, d), new_cache_k, new_cache_v)
```

```python
import functools

import jax
import jax.numpy as jnp
from jax import lax
from jax.experimental import pallas as pl
from jax.experimental.pallas import tpu as pltpu

F32 = jnp.float32
BF16 = jnp.bfloat16
I32 = jnp.int32

D_MODEL = 2048
N_HEADS = 8
HEAD_DIM = 128
D_A = N_HEADS * HEAD_DIM
N_GROUPS = 8
GROUP = 128
D_B = N_GROUPS * GROUP
CHUNK = 128
GRID_W = 64
WIN_ROWS = 8
WIN_COLS = 16
CONV_W = 31
CONV_PAD = 16
N_EXPERTS = 8
EPS = 1e-6
NEG_INF = -1e30
ATTN_SCALE = HEAD_DIM ** -0.5
N_MOD_ROWS = 16
LANE = 128
D_FF_PAD = 5632

VMEM_LIMIT = 60 * 1024 * 1024

TM = 1024
TN = 1024
TM_FFN = 512
TF_FFN = 512
TF_MOE = 1024
TM_ROUTE = 512
TM_MOVE = 256
ROW_CHUNK = 16
EPI_ROWS = 128


def _params(*sem):
    return pltpu.CompilerParams(dimension_semantics=sem, vmem_limit_bytes=VMEM_LIMIT)


def _modulate(x, g, shift, scale):
    ms = jnp.mean(x * x, axis=-1, keepdims=True)
    return ((x * lax.rsqrt(ms + EPS)) * g) * (1.0 + scale) + shift


def _row_loop(n_rows, fn):
    def body(c, carry):
        fn(pl.ds(pl.multiple_of(c * EPI_ROWS, EPI_ROWS), EPI_ROWS))
        return carry

    lax.fori_loop(0, n_rows // EPI_ROWS, body, 0)


def _lane_tiles(width):
    return [slice(c * LANE, (c + 1) * LANE) for c in range(width // LANE)]


def _row_mean(load, tiles):
    acc = None
    for cols in tiles:
        v = load(cols)
        acc = v if acc is None else acc + v
    return jnp.sum(acc, axis=-1, keepdims=True) / float(len(tiles) * LANE)


def _fill_modulated(x_ref, g_ref, sh_ref, sc_ref, h_ref):
    tiles = _lane_tiles(x_ref.shape[1])

    def fn(rows):
        ms = _row_mean(lambda cols: jnp.square(x_ref[rows, cols]), tiles)
        inv = lax.rsqrt(ms + EPS)
        for cols in tiles:
            y = (x_ref[rows, cols] * inv) * g_ref[:, cols]
            h_ref[rows, cols] = (y * (1.0 + sc_ref[:, cols]) + sh_ref[:, cols]).astype(h_ref.dtype)

    _row_loop(x_ref.shape[0], fn)


def _mod_spec(layer, which, row_of, width=D_MODEL, col_of=None):
    if col_of is None:
        return pl.BlockSpec((None, None, None, 1, width), lambda i, *_: (layer, row_of(i), which, 0, 0))
    return pl.BlockSpec((None, None, None, 1, width), lambda i, j, *_: (layer, row_of(i), which, 0, col_of(j)))


CAST_BLOCK_BYTES = 8 * 1024 * 1024


def _cast_kernel(w_ref, o_ref):
    o_ref[...] = w_ref[...].astype(o_ref.dtype)


def _cast_rows(rows, cols):
    best = 16
    for tr in range(16, rows + 1, 16):
        if rows % tr == 0 and tr * cols * 4 <= CAST_BLOCK_BYTES:
            best = tr
    return best


def _layer_bf16(w, layer):
    rows, cols = w.shape[-2:]
    tr = _cast_rows(rows, cols)
    if w.ndim == 3:
        grid = (rows // tr,)
        in_spec = pl.BlockSpec((None, tr, cols), lambda r: (layer, r, 0))
        out_spec = pl.BlockSpec((tr, cols), lambda r: (r, 0))
        sem = ("parallel",)
    else:
        grid = (w.shape[1], rows // tr)
        in_spec = pl.BlockSpec((None, None, tr, cols), lambda e, r: (layer, e, r, 0))
        out_spec = pl.BlockSpec((None, tr, cols), lambda e, r: (e, r, 0))
        sem = ("parallel", "parallel")
    return pl.pallas_call(
        _cast_kernel,
        grid=grid,
        in_specs=[in_spec],
        out_specs=out_spec,
        out_shape=jax.ShapeDtypeStruct(w.shape[1:], BF16),
        compiler_params=_params(*sem),
        name="weights_to_bf16",
    )(w)


def _mod_kernel(c_ref, w_ref, b_ref, o_ref):
    c = c_ref[...]
    sc = (c * jax.nn.sigmoid(c)).astype(BF16)
    o_ref[...] = jnp.dot(sc, w_ref[...].astype(BF16), preferred_element_type=F32) + b_ref[...]


def _modulation(cvec, w_mod, b_mod):
    depth, d, n = w_mod.shape
    tn = 1024
    out = pl.pallas_call(
        _mod_kernel,
        grid=(depth, n // tn),
        in_specs=[
            pl.BlockSpec((N_MOD_ROWS, d), lambda l, j: (0, 0)),
            pl.BlockSpec((None, d, tn), lambda l, j: (l, 0, j)),
            pl.BlockSpec((None, 1, tn), lambda l, j: (l, 0, j)),
        ],
        out_specs=pl.BlockSpec((None, N_MOD_ROWS, tn), lambda l, j: (l, 0, j)),
        out_shape=jax.ShapeDtypeStruct((depth, N_MOD_ROWS, n), F32),
        compiler_params=_params("parallel", "parallel"),
        name="modulation",
    )(cvec, w_mod, b_mod.reshape(depth, 1, n))
    return out.reshape(depth, N_MOD_ROWS, 6, 1, d)


def _head_rms(t, g):
    ms = jnp.mean(t * t, axis=-1, keepdims=True)
    return (t * lax.rsqrt(ms + EPS)) * g


def _even_proj_kernel(x_ref, g_ref, sh_ref, sc_ref, w_ref, qg_ref, kg_ref, lng_ref, lnb_ref, *rest, with_kv):
    if with_kv:
        p_ref, kv_ref, h_ref, acc_ref = rest
    else:
        p_ref, h_ref, acc_ref = rest
        kv_ref = None
    j = pl.program_id(1)

    @pl.when(j == 0)
    def _():
        _fill_modulated(x_ref, g_ref, sh_ref, sc_ref, h_ref)

    acc_ref[...] = jnp.dot(h_ref[...], w_ref[...], preferred_element_type=F32)
    n_rows = acc_ref.shape[0]
    tiles = _lane_tiles(acc_ref.shape[1])

    @pl.when(j == 0)
    def _():
        def fn(rows):
            for cols in tiles:
                q = _head_rms(acc_ref[rows, cols], qg_ref[...]) * ATTN_SCALE
                p_ref[rows, cols] = q.astype(BF16)
        _row_loop(n_rows, fn)

    @pl.when(j == 1)
    def _():
        def fn(rows):
            for cols in tiles:
                k = _head_rms(acc_ref[rows, cols], kg_ref[...])
                p_ref[rows, cols] = k.astype(BF16)
                if with_kv:
                    kv_ref[rows, cols] = k
        _row_loop(n_rows, fn)

    @pl.when(j == 2)
    def _():
        def fn(rows):
            for cols in tiles:
                v = acc_ref[rows, cols]
                p_ref[rows, cols] = v.astype(BF16)
                if with_kv:
                    kv_ref[rows, cols] = v
        _row_loop(n_rows, fn)

    @pl.when(j == 3)
    def _():
        def fn(rows):
            for cols in tiles:
                p_ref[rows, cols] = jax.nn.gelu(acc_ref[rows, cols]).astype(BF16)
        _row_loop(n_rows, fn)

    @pl.when(j == 4)
    def _():
        def fn(rows):
            def activated(cols):
                t = jax.nn.gelu(acc_ref[rows, cols])
                acc_ref[rows, cols] = t
                return t
            mu = _row_mean(activated, tiles)
            var = _row_mean(lambda cols: jnp.square(acc_ref[rows, cols] - mu), tiles)
            inv = lax.rsqrt(var + EPS)
            for cols in tiles:
                y = (acc_ref[rows, cols] - mu) * inv
                p_ref[rows, cols] = (y * lng_ref[:, cols] + lnb_ref[:, cols]).astype(BF16)
        _row_loop(n_rows, fn)


def _even_projection(x, mods, layer, row_of, g1, w_in, qg, kg, lng, lnb, with_kv):
    m, d = x.shape
    n = w_in.shape[1]
    out_shape = [jax.ShapeDtypeStruct((m, n), BF16)]
    out_specs = [pl.BlockSpec((TM, TN), lambda i, j: (i, j))]
    if with_kv:
        out_shape.append(jax.ShapeDtypeStruct((m, 2 * D_A), F32))
        out_specs.append(pl.BlockSpec((TM, TN), lambda i, j: (i, jnp.clip(j - 1, 0, 1))))
    vec = lambda w: pl.BlockSpec((1, w), lambda i, j: (0, 0))
    res = pl.pallas_call(
        functools.partial(_even_proj_kernel, with_kv=with_kv),
        grid=(m // TM, n // TN),
        in_specs=[
            pl.BlockSpec((TM, d), lambda i, j: (i, 0)),
            vec(d),
            _mod_spec(layer, 0, row_of),
            _mod_spec(layer, 1, row_of),
            pl.BlockSpec((d, TN), lambda i, j: (0, j)),
            vec(HEAD_DIM), vec(HEAD_DIM), vec(D_B), vec(D_B),
        ],
        out_specs=out_specs,
        out_shape=out_shape,
        scratch_shapes=[pltpu.VMEM((TM, d), BF16), pltpu.VMEM((TM, TN), F32)],
        compiler_params=_params("parallel", "arbitrary"),
        name="even_projection",
    )(x, g1, mods, mods, w_in, qg, kg, lng, lnb)
    return res if with_kv else (res[0], None)


_NT = (((1,), (1,)), ((), ()))


def _ctx_attn_kernel(q_ref, k_ref, v_ref, o_ref):
    for h in range(N_HEADS):
        cols = slice(h * HEAD_DIM, (h + 1) * HEAD_DIM)
        s = lax.dot_general(q_ref[:, cols], k_ref[:, cols], _NT, preferred_element_type=F32)
        m = jnp.max(s, axis=-1, keepdims=True)
        p = jnp.exp(s - m)
        l = jnp.sum(p, axis=-1, keepdims=True)
        o = jnp.dot(p.astype(BF16), v_ref[:, cols], preferred_element_type=F32)
        o_ref[:, cols] = (o / l).astype(o_ref.dtype)


def _context_attention(p, batch, seq):
    return pl.pallas_call(
        _ctx_attn_kernel,
        grid=(batch,),
        in_specs=[
            pl.BlockSpec((seq, D_A), lambda b: (b, 0)),
            pl.BlockSpec((seq, D_A), lambda b: (b, 1)),
            pl.BlockSpec((seq, D_A), lambda b: (b, 2)),
        ],
        out_specs=pl.BlockSpec((seq, D_A), lambda b: (b, 0)),
        out_shape=jax.ShapeDtypeStruct((batch * seq, D_A), BF16),
        compiler_params=_params("parallel"),
        name="context_attention",
    )(p, p, p)


def _bias_kernel(rpb_ref, o_ref):
    v = pl.program_id(0)
    q = lax.broadcasted_iota(I32, (GRID_W, GRID_W), 0)
    k = lax.broadcasted_iota(I32, (GRID_W, GRID_W), 1)
    dc = jnp.clip(k - q, -(WIN_COLS - 1), WIN_COLS - 1) + (WIN_COLS - 1)
    col_start = jnp.clip(q - WIN_COLS // 2, 0, GRID_W - WIN_COLS)
    col_in = (k >= col_start) & (k < col_start + WIN_COLS)
    n_dr, n_dc = 2 * WIN_ROWS - 1, 2 * WIN_COLS - 1
    for h in range(N_HEADS):
        for i in range(WIN_ROWS):
            base = (h * n_dr + v + i) * n_dc
            tile = jnp.full((GRID_W, GRID_W), NEG_INF, F32)
            for d in range(n_dc):
                tile = jnp.where(dc == d, rpb_ref[base + d], tile)
            o_ref[h, i] = jnp.where(col_in, tile, NEG_INF)


def _bias_table(rpb_e):
    tab = pl.pallas_call(
        _bias_kernel,
        grid=(WIN_ROWS,),
        in_specs=[pl.BlockSpec(memory_space=pltpu.SMEM)],
        out_specs=pl.BlockSpec((None, N_HEADS, WIN_ROWS, GRID_W, GRID_W), lambda v: (v, 0, 0, 0, 0)),
        out_shape=jax.ShapeDtypeStruct((WIN_ROWS, N_HEADS, WIN_ROWS, GRID_W, GRID_W), F32),
        compiler_params=_params("parallel"),
        name="rpb_table",
    )(rpb_e.reshape(-1))
    return tab.transpose(0, 1, 3, 2, 4).reshape(WIN_ROWS, N_HEADS, GRID_W, WIN_ROWS * GRID_W)


def _nbr_attn_kernel(q_ref, k_ref, v_ref, ck_ref, cv_ref, b_ref, o_ref, s_ref, p_ref, *, rows):
    r = pl.program_id(1)
    row_start = jnp.clip(r - WIN_ROWS // 2, 0, rows - WIN_ROWS)
    k0 = pl.multiple_of(row_start * GRID_W, GRID_W)
    n_loc = WIN_ROWS * GRID_W
    heads = [slice(h * HEAD_DIM, (h + 1) * HEAD_DIM) for h in range(N_HEADS)]
    for h, cols in enumerate(heads):
        q = q_ref[:, cols]
        s_ref[h, :, :n_loc] = (
            lax.dot_general(q, k_ref[pl.ds(k0, n_loc), cols], _NT, preferred_element_type=F32) + b_ref[h])
        s_ref[h, :, n_loc:] = lax.dot_general(q, ck_ref[:, cols], _NT, preferred_element_type=F32)
    denom = []
    for h in range(N_HEADS):
        s = s_ref[h]
        p = jnp.exp(s - jnp.max(s, axis=-1, keepdims=True))
        denom.append(jnp.sum(p, axis=-1, keepdims=True))
        p_ref[h] = p.astype(BF16)
    for h, cols in enumerate(heads):
        o = (jnp.dot(p_ref[h, :, :n_loc], v_ref[pl.ds(k0, n_loc), cols], preferred_element_type=F32)
             + jnp.dot(p_ref[h, :, n_loc:], cv_ref[:, cols], preferred_element_type=F32))
        o_ref[:, cols] = (o / denom[h]).astype(o_ref.dtype)


def _neighbourhood_attention(p, ck, cv, bias_tab, batch, seq):
    rows = seq // GRID_W
    past = ck.shape[1]

    def placement(r):
        return jnp.clip(r - WIN_ROWS // 2, 0, rows - WIN_ROWS) - r + (WIN_ROWS - 1)

    return pl.pallas_call(
        functools.partial(_nbr_attn_kernel, rows=rows),
        grid=(batch, rows),
        in_specs=[
            pl.BlockSpec((GRID_W, D_A), lambda b, r: (b * rows + r, 0)),
            pl.BlockSpec((seq, D_A), lambda b, r: (b, 1)),
            pl.BlockSpec((seq, D_A), lambda b, r: (b, 2)),
            pl.BlockSpec((None, past, D_A), lambda b, r: (b, 0, 0)),
            pl.BlockSpec((None, past, D_A), lambda b, r: (b, 0, 0)),
            pl.BlockSpec((None, N_HEADS, GRID_W, WIN_ROWS * GRID_W), lambda b, r: (placement(r), 0, 0, 0)),
        ],
        out_specs=pl.BlockSpec((GRID_W, D_A), lambda b, r: (b * rows + r, 0)),
        out_shape=jax.ShapeDtypeStruct((batch * seq, D_A), BF16),
        scratch_shapes=[
            pltpu.VMEM((N_HEADS, GRID_W, WIN_ROWS * GRID_W + past), F32),
            pltpu.VMEM((N_HEADS, GRID_W, WIN_ROWS * GRID_W + past), BF16),
        ],
        compiler_params=_params("parallel", "arbitrary"),
        name="neighbourhood_attention",
    )(p, p, p, ck, cv, bias_tab)


def _sgu_kernel(u_ref, v_ref, w_ref, bt_ref, o_ref):
    for n in range(u_ref.shape[0] // CHUNK):
        rows = slice(n * CHUNK, (n + 1) * CHUNK)
        for g in range(N_GROUPS):
            cols = slice(g * GROUP, (g + 1) * GROUP)
            s = jnp.dot(w_ref[g], v_ref[rows, cols], preferred_element_type=F32) + bt_ref[:, g:g + 1]
            o_ref[rows, cols] = (u_ref[rows, cols].astype(F32) * s).astype(o_ref.dtype)


def _spatial_gating(p, w_s, b_s_t):
    m = p.shape[0]
    tq = 4 * CHUNK
    return pl.pallas_call(
        _sgu_kernel,
        grid=(m // tq,),
        in_specs=[
            pl.BlockSpec((tq, D_B), lambda i: (i, 3)),
            pl.BlockSpec((tq, D_B), lambda i: (i, 4)),
            pl.BlockSpec((N_GROUPS, CHUNK, CHUNK), lambda i: (0, 0, 0)),
            pl.BlockSpec((CHUNK, N_GROUPS), lambda i: (0, 0)),
        ],
        out_specs=pl.BlockSpec((tq, D_B), lambda i: (i, 0)),
        out_shape=jax.ShapeDtypeStruct((m, D_B), BF16),
        compiler_params=_params("parallel"),
        name="spatial_gating",
    )(p, p, w_s, b_s_t)


def _out_proj_kernel(x_ref, a_ref, s_ref, w_ref, gate_ref, o_ref):
    acc = jnp.dot(a_ref[...], w_ref[:D_A, :], preferred_element_type=F32)
    acc = acc + jnp.dot(s_ref[...], w_ref[D_A:, :], preferred_element_type=F32)
    o_ref[...] = x_ref[...] + gate_ref[...] * acc


def _out_projection(x, a, s, w_out, mods, layer, row_of):
    m, d = x.shape
    return pl.pallas_call(
        _out_proj_kernel,
        grid=(m // TM, d // TN),
        in_specs=[
            pl.BlockSpec((TM, TN), lambda i, j: (i, j)),
            pl.BlockSpec((TM, D_A), lambda i, j: (i, 0)),
            pl.BlockSpec((TM, D_B), lambda i, j: (i, 0)),
            pl.BlockSpec((D_A + D_B, TN), lambda i, j: (0, j)),
            _mod_spec(layer, 2, row_of, TN, lambda j: j),
        ],
        out_specs=pl.BlockSpec((TM, TN), lambda i, j: (i, j)),
        out_shape=jax.ShapeDtypeStruct((m, d), F32),
        input_output_aliases={0: 0},
        compiler_params=_params("parallel", "parallel"),
        name="even_out_projection",
    )(x, a, s, w_out, mods)


def _ffn_kernel(x_ref, g_ref, sh_ref, sc_ref, wg_ref, wu_ref, wd_ref, gate_ref, o_ref, h_ref, acc_ref):
    f = pl.program_id(1)

    @pl.when(f == 0)
    def _():
        _fill_modulated(x_ref, g_ref, sh_ref, sc_ref, h_ref)

    h = h_ref[...]
    a = jnp.dot(h, wg_ref[...], preferred_element_type=F32)
    u = jnp.dot(h, wu_ref[...], preferred_element_type=F32)
    y = jnp.dot((jax.nn.silu(a) * u).astype(BF16), wd_ref[...], preferred_element_type=F32)

    @pl.when(f == 0)
    def _():
        acc_ref[...] = y

    @pl.when(f > 0)
    def _():
        acc_ref[...] += y

    @pl.when(f == pl.num_programs(1) - 1)
    def _():
        o_ref[...] = x_ref[...] + gate_ref[...] * acc_ref[...]


def _dense_ffn(x, mods, layer, row_of, g2, wg, wu, wd):
    m, d = x.shape
    ff = wg.shape[1]
    row_tile = lambda i: row_of(i)
    return pl.pallas_call(
        _ffn_kernel,
        grid=(m // TM_FFN, ff // TF_FFN),
        in_specs=[
            pl.BlockSpec((TM_FFN, d), lambda i, f: (i, 0)),
            pl.BlockSpec((1, d), lambda i, f: (0, 0)),
            _mod_spec(layer, 3, row_tile),
            _mod_spec(layer, 4, row_tile),
            pl.BlockSpec((d, TF_FFN), lambda i, f: (0, f)),
            pl.BlockSpec((d, TF_FFN), lambda i, f: (0, f)),
            pl.BlockSpec((TF_FFN, d), lambda i, f: (f, 0)),
            _mod_spec(layer, 5, row_tile),
        ],
        out_specs=pl.BlockSpec((TM_FFN, d), lambda i, f: (i, 0)),
        out_shape=jax.ShapeDtypeStruct((m, d), F32),
        scratch_shapes=[pltpu.VMEM((TM_FFN, d), BF16), pltpu.VMEM((TM_FFN, d), F32)],
        input_output_aliases={0: 0},
        compiler_params=_params("parallel", "arbitrary"),
        name="dense_swiglu",
    )(x, g2, mods, mods, wg, wu, wd, mods)


def _glu_kernel(x_ref, g_ref, sh_ref, sc_ref, wa_ref, wg_ref, ba_ref, bg_ref, z_ref, h_ref):
    @pl.when(pl.program_id(1) == 0)
    def _():
        _fill_modulated(x_ref, g_ref, sh_ref, sc_ref, h_ref)

    h = h_ref[...]
    a = jnp.dot(h, wa_ref[...], preferred_element_type=F32) + ba_ref[...]
    gt = jnp.dot(h, wg_ref[...], preferred_element_type=F32) + bg_ref[...]
    z_ref[...] = a * jax.nn.sigmoid(gt)


def _conv_glu(x, mods, layer, row_of, g1, w1, b1):
    m, d = x.shape
    nj = d // TN
    return pl.pallas_call(
        _glu_kernel,
        grid=(m // TM, nj),
        in_specs=[
            pl.BlockSpec((TM, d), lambda i, j: (i, 0)),
            pl.BlockSpec((1, d), lambda i, j: (0, 0)),
            _mod_spec(layer, 0, row_of),
            _mod_spec(layer, 1, row_of),
            pl.BlockSpec((d, TN), lambda i, j: (0, j)),
            pl.BlockSpec((d, TN), lambda i, j: (0, j + nj)),
            pl.BlockSpec((1, TN), lambda i, j: (0, j)),
            pl.BlockSpec((1, TN), lambda i, j: (0, j + nj)),
        ],
        out_specs=pl.BlockSpec((TM, TN), lambda i, j: (i, j)),
        out_shape=jax.ShapeDtypeStruct((m, d), F32),
        scratch_shapes=[pltpu.VMEM((TM, d), BF16)],
        compiler_params=_params("parallel", "arbitrary"),
        name="conv_glu",
    )(x, g1, mods, mods, w1, w1, b1, b1)


def _dwconv_kernel(z_ref, w_ref, b_ref, o_ref, pad_ref):
    seq, tc = z_ref.shape
    rc = 64
    zeros = jnp.zeros((CONV_PAD, tc), F32)
    pad_ref[0:CONV_PAD, :] = zeros
    pad_ref[CONV_PAD + seq:, :] = zeros

    def copy(c, carry):
        r0 = pl.multiple_of(c * rc, rc)
        pad_ref[pl.ds(CONV_PAD + r0, rc), :] = z_ref[pl.ds(r0, rc), :]
        return carry

    lax.fori_loop(0, seq // rc, copy, 0)
    bias = jnp.broadcast_to(b_ref[...], (rc, tc))
    first = CONV_PAD - CONV_W // 2
    win_rows = rc + 2 * CONV_PAD
    sub = 8

    def conv(c, carry):
        r0 = pl.multiple_of(c * rc, rc)
        win = pad_ref[pl.ds(r0, win_rows), :]
        acc = bias
        for s in range(sub):
            shifted = win if s == 0 else pltpu.roll(win, win_rows - s, 0)
            for k in range(CONV_W):
                off = first + k
                if off % sub == s:
                    a0 = off - s
                    acc = acc + shifted[a0:a0 + rc, :] * w_ref[k:k + 1, :]
        o_ref[pl.ds(r0, rc), :] = acc
        return carry

    lax.fori_loop(0, seq // rc, conv, 0)


def _depthwise_conv(z, batch, seq, dw, dw_b):
    d = z.shape[1]
    tc = 256
    out = pl.pallas_call(
        _dwconv_kernel,
        grid=(batch, d // tc),
        in_specs=[
            pl.BlockSpec((None, seq, tc), lambda b, c: (b, 0, c)),
            pl.BlockSpec((CONV_W, tc), lambda b, c: (0, c)),
            pl.BlockSpec((1, tc), lambda b, c: (0, c)),
        ],
        out_specs=pl.BlockSpec((None, seq, tc), lambda b, c: (b, 0, c)),
        out_shape=jax.ShapeDtypeStruct((batch, seq, d), F32),
        scratch_shapes=[pltpu.VMEM((seq + 2 * CONV_PAD, tc), F32)],
        compiler_params=_params("parallel", "parallel"),
        name="depthwise_conv",
    )(z.reshape(batch, seq, d), dw, dw_b)
    return out.reshape(batch * seq, d)


def _conv_out_kernel(z_ref, lng_ref, lnb_ref, w_ref, b_ref, x_ref, gate_ref, o_ref, h_ref):
    @pl.when(pl.program_id(1) == 0)
    def _():
        tiles = _lane_tiles(z_ref.shape[1])

        def fn(rows):
            mu = _row_mean(lambda cols: z_ref[rows, cols], tiles)
            var = _row_mean(lambda cols: jnp.square(z_ref[rows, cols] - mu), tiles)
            inv = lax.rsqrt(var + EPS)
            for cols in tiles:
                y = ((z_ref[rows, cols] - mu) * inv) * lng_ref[:, cols] + lnb_ref[:, cols]
                h_ref[rows, cols] = jax.nn.silu(y).astype(BF16)

        _row_loop(z_ref.shape[0], fn)

    acc = jnp.dot(h_ref[...], w_ref[...], preferred_element_type=F32) + b_ref[...]
    o_ref[...] = x_ref[...] + gate_ref[...] * acc


def _conv_out(zc, x, mods, layer, row_of, ln_g, ln_b, w2, b2):
    m, d = x.shape
    return pl.pallas_call(
        _conv_out_kernel,
        grid=(m // TM, d // TN),
        in_specs=[
            pl.BlockSpec((TM, d), lambda i, j: (i, 0)),
            pl.BlockSpec((1, d), lambda i, j: (0, 0)),
            pl.BlockSpec((1, d), lambda i, j: (0, 0)),
            pl.BlockSpec((d, TN), lambda i, j: (0, j)),
            pl.BlockSpec((1, TN), lambda i, j: (0, j)),
            pl.BlockSpec((TM, TN), lambda i, j: (i, j)),
            _mod_spec(layer, 2, row_of, TN, lambda j: j),
        ],
        out_specs=pl.BlockSpec((TM, TN), lambda i, j: (i, j)),
        out_shape=jax.ShapeDtypeStruct((m, d), F32),
        scratch_shapes=[pltpu.VMEM((TM, d), BF16)],
        input_output_aliases={5: 0},
        compiler_params=_params("parallel", "arbitrary"),
        name="conv_out_projection",
    )(zc, ln_g, ln_b, w2, b2, x, mods)


PAGE = D_MODEL // LANE


def _store_pages(ref, t0, val):
    n = val.shape[0]
    for s in range(PAGE):
        ref[pl.ds(t0 * PAGE + s, n, stride=PAGE), :] = val[:, s * LANE:(s + 1) * LANE]


def _load_page_tile(ref, n, s):
    return ref[pl.ds(s, n, stride=PAGE), :]


def _token_page(ref, t):
    return ref.at[pl.ds(pl.multiple_of(t * PAGE, PAGE), PAGE)]


def _router_kernel(x_ref, g_ref, sh_ref, sc_ref, rw_ref, base_ref, hp_ref, ri_ref, rg_ref, cnt_ref,
                   run_ref, tri_ref, h_ref):
    tm = x_ref.shape[0]

    @pl.when(pl.program_id(0) == 0)
    def _():
        run_ref[...] = base_ref[...]
        src = lax.broadcasted_iota(I32, (tm, tm), 0)
        dst = lax.broadcasted_iota(I32, (tm, tm), 1)
        tri_ref[...] = jnp.where(src < dst, 1.0, 0.0).astype(BF16)

    g, sh, sc = g_ref[...], sh_ref[...], sc_ref[...]
    rc = 2 * ROW_CHUNK
    for c in range(tm // rc):
        h = _modulate(x_ref[c * rc:(c + 1) * rc, :], g, sh, sc)
        h_ref[c * rc:(c + 1) * rc, :] = h
        _store_pages(hp_ref, c * rc, h)
    logits = lax.dot_general(rw_ref[...], h_ref[...], _NT, precision=lax.Precision.HIGHEST,
                             preferred_element_type=F32)
    eid = lax.broadcasted_iota(I32, logits.shape, 0).astype(F32)
    none = float(N_EXPERTS)
    m1 = jnp.max(logits, axis=0, keepdims=True)
    i1 = jnp.min(jnp.where(logits == m1, eid, none), axis=0, keepdims=True)
    rest = jnp.where(eid == i1, -jnp.inf, logits)
    m2 = jnp.max(rest, axis=0, keepdims=True)
    i2 = jnp.min(jnp.where(rest == m2, eid, none), axis=0, keepdims=True)
    e2 = jnp.exp(m2 - m1)
    gate1 = 1.0 / (1.0 + e2)
    gate2 = e2 / (1.0 + e2)
    hot1 = eid == i1
    hot2 = eid == i2
    chosen = jnp.where(hot1, 1.0, jnp.where(hot2, 1.0, 0.0))
    before = jnp.dot(chosen.astype(BF16), tri_ref[...], preferred_element_type=F32) + run_ref[:, 0:1]
    rank1 = jnp.sum(jnp.where(hot1, before, 0.0), axis=0, keepdims=True)
    rank2 = jnp.sum(jnp.where(hot2, before, 0.0), axis=0, keepdims=True)
    ri_ref[...] = jnp.zeros(ri_ref.shape, I32)
    ri_ref[0:1, :] = i1.astype(I32)
    ri_ref[1:2, :] = i2.astype(I32)
    ri_ref[2:3, :] = rank1.astype(I32)
    ri_ref[3:4, :] = rank2.astype(I32)
    rg_ref[...] = jnp.zeros(rg_ref.shape, F32)
    rg_ref[0:1, :] = gate1
    rg_ref[1:2, :] = gate2
    run_ref[...] = run_ref[...] + jnp.sum(chosen, axis=1, keepdims=True)
    cnt_ref[...] = run_ref[...]


def _router(x, mods, layer, row_of, g2, rw_t, base):
    m, d = x.shape
    tm = TM_ROUTE
    return pl.pallas_call(
        _router_kernel,
        grid=(m // tm,),
        in_specs=[
            pl.BlockSpec((tm, d), lambda i: (i, 0)),
            pl.BlockSpec((1, d), lambda i: (0, 0)),
            _mod_spec(layer, 3, row_of),
            _mod_spec(layer, 4, row_of),
            pl.BlockSpec((N_EXPERTS, d), lambda i: (0, 0)),
            pl.BlockSpec((N_EXPERTS, LANE), lambda i: (0, 0)),
        ],
        out_specs=[
            pl.BlockSpec((tm * PAGE, LANE), lambda i: (i, 0)),
            pl.BlockSpec((8, tm), lambda i: (0, i)),
            pl.BlockSpec((8, tm), lambda i: (0, i)),
            pl.BlockSpec((N_EXPERTS, LANE), lambda i: (0, 0)),
        ],
        out_shape=[
            jax.ShapeDtypeStruct((m * PAGE, LANE), F32),
            jax.ShapeDtypeStruct((8, m), I32),
            jax.ShapeDtypeStruct((8, m), F32),
            jax.ShapeDtypeStruct((N_EXPERTS, LANE), F32),
        ],
        scratch_shapes=[pltpu.VMEM((N_EXPERTS, LANE), F32), pltpu.VMEM((tm, tm), BF16), pltpu.VMEM((tm, d), F32)],
        compiler_params=_params("arbitrary"),
        name="moe_router",
    )(x, g2, mods, mods, rw_t, base)


def _dispatch_kernel(pos_ref, h_ref, xs_in_ref, xs_ref, sem):
    del xs_in_ref
    tm = TM_MOVE
    n_tok = pos_ref.shape[0] // 2
    t0 = pl.program_id(0) * tm

    def row_copy(t, k):
        dst = pos_ref[k * n_tok + t0 + t]
        return pltpu.make_async_copy(_token_page(h_ref, t0 + t), _token_page(xs_ref, dst), sem)

    def start(t, carry):
        row_copy(t, 0).start()
        row_copy(t, 1).start()
        return carry

    def wait(t, carry):
        row_copy(t, 0).wait()
        row_copy(t, 1).wait()
        return carry

    lax.fori_loop(0, tm, start, 0, unroll=4)
    lax.fori_loop(0, tm, wait, 0, unroll=4)


def _dispatch(pos_flat, h, xs):
    m = h.shape[0] // PAGE
    return pl.pallas_call(
        _dispatch_kernel,
        grid_spec=pltpu.PrefetchScalarGridSpec(
            num_scalar_prefetch=1,
            grid=(m // TM_MOVE,),
            in_specs=[
                pl.BlockSpec(memory_space=pl.ANY),
                pl.BlockSpec(memory_space=pl.ANY),
            ],
            out_specs=pl.BlockSpec(memory_space=pl.ANY),
            scratch_shapes=[pltpu.SemaphoreType.DMA(())],
        ),
        out_shape=jax.ShapeDtypeStruct(xs.shape, xs.dtype),
        input_output_aliases={2: 0},
        compiler_params=_params("arbitrary"),
        name="moe_dispatch",
    )(pos_flat, h, xs)


def _expert_kernel(te_ref, nu_ref, xs_ref, wg_ref, wu_ref, wd_ref, ys_ref, xb_ref, acc_ref):
    t = pl.program_id(0)
    f = pl.program_id(1)
    last = pl.num_programs(1) - 1
    used = t < nu_ref[0]

    tm = xb_ref.shape[0]

    @pl.when(used & (f == 0))
    def _():
        for s in range(PAGE):
            xb_ref[:, s * LANE:(s + 1) * LANE] = _load_page_tile(xs_ref, tm, s).astype(BF16)

    @pl.when(used)
    def _():
        h = xb_ref[...]
        a = jnp.dot(h, wg_ref[...], preferred_element_type=F32)
        u = jnp.dot(h, wu_ref[...], preferred_element_type=F32)
        y = jnp.dot((jax.nn.silu(a) * u).astype(BF16), wd_ref[...], preferred_element_type=F32)

        @pl.when(f == 0)
        def _():
            acc_ref[...] = y

        @pl.when(f > 0)
        def _():
            acc_ref[...] += y

    @pl.when(used & (f == last))
    def _():
        for s in range(PAGE):
            ys_ref[pl.ds(s, tm, stride=PAGE), :] = acc_ref[:, s * LANE:(s + 1) * LANE]

    @pl.when(jnp.logical_not(used) & (f == last))
    def _():
        ys_ref[...] = jnp.zeros(ys_ref.shape, ys_ref.dtype)


def _experts(tile_expert, n_used, xs, wg, wu, wd):
    p, d = xs.shape[0] // PAGE, D_MODEL
    ff = wg.shape[2]
    nf = ff // TF_MOE

    def row_idx(t, f, te, nu):
        return (jnp.minimum(t, nu[0] - 1), 0)

    def hid(t, f, nu):
        return jnp.where(t < nu[0], f, nf - 1)

    return pl.pallas_call(
        _expert_kernel,
        grid_spec=pltpu.PrefetchScalarGridSpec(
            num_scalar_prefetch=2,
            grid=(p // TM_FFN, nf),
            in_specs=[
                pl.BlockSpec((TM_FFN * PAGE, LANE), row_idx),
                pl.BlockSpec((None, d, TF_MOE), lambda t, f, te, nu: (te[t], 0, hid(t, f, nu))),
                pl.BlockSpec((None, d, TF_MOE), lambda t, f, te, nu: (te[t], 0, hid(t, f, nu))),
                pl.BlockSpec((None, TF_MOE, d), lambda t, f, te, nu: (te[t], hid(t, f, nu), 0)),
            ],
            out_specs=pl.BlockSpec((TM_FFN * PAGE, LANE), lambda t, f, te, nu: (t, 0)),
            scratch_shapes=[pltpu.VMEM((TM_FFN, d), BF16), pltpu.VMEM((TM_FFN, d), F32)],
        ),
        out_shape=jax.ShapeDtypeStruct((p * PAGE, LANE), F32),
        compiler_params=_params("arbitrary", "arbitrary"),
        name="moe_experts",
    )(tile_expert, n_used, xs, wg, wu, wd)


def _combine_kernel(pos_ref, ys_ref, x_ref, gt_ref, gate_ref, o_ref, buf_ref, sem):
    tm = x_ref.shape[0]
    n_tok = pos_ref.shape[0] // 2
    t0 = pl.program_id(0) * tm

    def row_copy(t, k):
        src = pos_ref[k * n_tok + t0 + t]
        return pltpu.make_async_copy(_token_page(ys_ref, src), _token_page(buf_ref.at[k], t), sem)

    def start(t, carry):
        row_copy(t, 0).start()
        row_copy(t, 1).start()
        return carry

    def wait(t, carry):
        row_copy(t, 0).wait()
        row_copy(t, 1).wait()
        return carry

    lax.fori_loop(0, tm, start, 0, unroll=4)
    lax.fori_loop(0, tm, wait, 0, unroll=4)
    g1 = gt_ref[:, 0:1]
    g2 = gt_ref[:, 1:2]
    for s in range(PAGE):
        cols = slice(s * LANE, (s + 1) * LANE)
        y = g1 * _load_page_tile(buf_ref.at[0], tm, s) + g2 * _load_page_tile(buf_ref.at[1], tm, s)
        o_ref[:, cols] = x_ref[:, cols] + gate_ref[:, cols] * y


def _combine(pos_flat, ys, x, gates_t, mods, layer, row_of):
    m, d = x.shape
    return pl.pallas_call(
        _combine_kernel,
        grid_spec=pltpu.PrefetchScalarGridSpec(
            num_scalar_prefetch=1,
            grid=(m // TM_MOVE,),
            in_specs=[
                pl.BlockSpec(memory_space=pl.ANY),
                pl.BlockSpec((TM_MOVE, d), lambda i, pos: (i, 0)),
                pl.BlockSpec((TM_MOVE, 2), lambda i, pos: (i, 0)),
                _mod_spec(layer, 5, row_of),
            ],
            out_specs=pl.BlockSpec((TM_MOVE, d), lambda i, pos: (i, 0)),
            scratch_shapes=[pltpu.VMEM((2, TM_MOVE * PAGE, LANE), F32), pltpu.SemaphoreType.DMA(())],
        ),
        out_shape=jax.ShapeDtypeStruct((m, d), F32),
        input_output_aliases={2: 0},
        compiler_params=_params("arbitrary"),
        name="moe_combine",
    )(pos_flat, ys, x, gates_t, mods)


def _moe(streams, mods, layer, g2, rw_t, wg, wu, wd):
    n_rows = sum(2 * x.shape[0] for x, _, _ in streams)
    n_tiles = n_rows // TM_FFN + N_EXPERTS
    base = jnp.zeros((N_EXPERTS, LANE), F32)
    routed = []
    for x, row_of, tile in streams:
        h, ri, rg, base = _router(x, mods, layer, lambda i, ro=row_of: ro(i, TM_ROUTE), g2, rw_t, base)
        routed.append((h, ri, rg))
    counts = base[:, 0].astype(I32)
    tiles_e = (counts + TM_FFN - 1) // TM_FFN
    tile_end = jnp.cumsum(tiles_e)
    row_off = (tile_end - tiles_e) * TM_FFN
    n_used = tile_end[-1:]
    tile_ids = jnp.minimum(jnp.arange(n_tiles, dtype=I32), n_used[0] - 1)
    tile_expert = jnp.sum(tile_ids[:, None] >= tile_end[None, :], axis=1).astype(I32)
    xs = jnp.zeros((n_tiles * TM_FFN * PAGE, LANE), F32)
    pos_all = []
    for h, ri, rg in routed:
        ids, rank = ri[0:2], ri[2:4]
        off = jnp.sum(jnp.where(ids[:, :, None] == jnp.arange(N_EXPERTS)[None, None, :], row_off[None, None, :], 0), axis=-1)
        pos = (off + rank).reshape(-1)
        pos_all.append(pos)
        xs = _dispatch(pos, h, xs)
    ys = _experts(tile_expert, n_used, xs, wg, wu, wd)
    outs = []
    for (x, row_of, tile), (h, ri, rg), pos in zip(streams, routed, pos_all):
        outs.append(_combine(pos, ys, x, rg[0:2].T, mods, layer, lambda i, ro=row_of: ro(i, TM_MOVE)))
    return outs


def kernel(x_prompt, x_sample, cache_k, cache_v, c, c_ctx, w_mod, b_mod, norm1_g, norm2_g, w_in_even, q_norm_g, k_norm_g, rpb, sgu_ln_g, sgu_ln_b, sgu_w, sgu_b, w_out_even, ffn_w_gate, ffn_w_up, ffn_w_down, conv_w1, conv_b1, conv_dw, conv_dw_b, conv_ln_g, conv_ln_b, conv_w2, conv_b2, router_w, moe_w_gate, moe_w_up, moe_w_down):
    batch, seq, d = x_prompt.shape
    dec_batch, dec_seq, _ = x_sample.shape
    depth = w_mod.shape[0]
    n_even = w_in_even.shape[0]
    past = cache_k.shape[2]
    xp = x_prompt.reshape(batch * seq, d)
    xs = x_sample.reshape(dec_batch * dec_seq, d)

    cvec = jnp.zeros((N_MOD_ROWS, d), F32).at[0].set(c_ctx).at[1:1 + dec_batch].set(c)
    mods = _modulation(cvec, w_mod, b_mod)

    prompt_row = lambda i, tile=TM: 0
    sample_row = lambda i, tile=TM: 1 + (i * tile) // dec_seq
    row2 = lambda a: a.reshape(1, -1)

    new_k, new_v = [], []
    for l in range(depth):
        g1, g2 = row2(norm1_g[l]), row2(norm2_g[l])
        if l % 2 == 0:
            e = l // 2
            w_in = _layer_bf16(w_in_even, e)
            w_out = _layer_bf16(w_out_even, e)
            w_s = sgu_w[e].astype(BF16)
            b_s_t = sgu_b[e].T
            ff_pad = D_FF_PAD - ffn_w_gate.shape[2]
            wg = jnp.pad(_layer_bf16(ffn_w_gate, e), ((0, 0), (0, ff_pad)))
            wu = jnp.pad(_layer_bf16(ffn_w_up, e), ((0, 0), (0, ff_pad)))
            wd = jnp.pad(_layer_bf16(ffn_w_down, e), ((0, ff_pad), (0, 0)))
            ck = cache_k[:, e].reshape(dec_batch, past, D_A).astype(BF16)
            cv = cache_v[:, e].reshape(dec_batch, past, D_A).astype(BF16)
            bias_tab = _bias_table(rpb[e])
            norm_args = (row2(q_norm_g[e]), row2(k_norm_g[e]), row2(sgu_ln_g[e]), row2(sgu_ln_b[e]))

            pp, kv = _even_projection(xp, mods, l, prompt_row, g1, w_in, *norm_args, with_kv=True)
            new_k.append(kv[:, :D_A].reshape(batch, seq, N_HEADS, HEAD_DIM))
            new_v.append(kv[:, D_A:].reshape(batch, seq, N_HEADS, HEAD_DIM))
            ap = _context_attention(pp, batch, seq)
            sp = _spatial_gating(pp, w_s, b_s_t)
            xp = _out_projection(xp, ap, sp, w_out, mods, l, prompt_row)

            ps, _ = _even_projection(xs, mods, l, sample_row, g1, w_in, *norm_args, with_kv=False)
            a_s = _neighbourhood_attention(ps, ck, cv, bias_tab, dec_batch, dec_seq)
            s_s = _spatial_gating(ps, w_s, b_s_t)
            xs = _out_projection(xs, a_s, s_s, w_out, mods, l, sample_row)

            xp = _dense_ffn(xp, mods, l, lambda i: prompt_row(i, TM_FFN), g2, wg, wu, wd)
            xs = _dense_ffn(xs, mods, l, lambda i: sample_row(i, TM_FFN), g2, wg, wu, wd)
        else:
            o = l // 2
            w1 = _layer_bf16(conv_w1, o)
            w2 = _layer_bf16(conv_w2, o)
            b1, b2 = row2(conv_b1[o]), row2(conv_b2[o])
            dw, dw_b = conv_dw[o], row2(conv_dw_b[o])
            ln_g, ln_b = row2(conv_ln_g[o]), row2(conv_ln_b[o])
            outs = []
            for x, row_of, nb, ns in ((xp, prompt_row, batch, seq), (xs, sample_row, dec_batch, dec_seq)):
                z = _conv_glu(x, mods, l, row_of, g1, w1, b1)
                zc = _depthwise_conv(z, nb, ns, dw, dw_b)
                outs.append(_conv_out(zc, x, mods, l, row_of, ln_g, ln_b, w2, b2))
            xp, xs = outs
            xp, xs = _moe(
                [(xp, prompt_row, None), (xs, sample_row, None)], mods, l, g2,
                router_w[o].T, _layer_bf16(moe_w_gate, o), _layer_bf16(moe_w_up, o), _layer_bf16(moe_w_down, o))

    new_cache_k = jnp.stack(new_k, axis=1)
    new_cache_v = jnp.stack(new_v, axis=1)
    return (xp.reshape(batch, seq, d), xs.reshape(dec_batch, dec_seq, d), new_cache_k, new_cache_v)
```

```python
import functools

import jax
import jax.numpy as jnp
from jax import lax
from jax.experimental import pallas as pl
from jax.experimental.pallas import tpu as pltpu

F32 = jnp.float32
BF16 = jnp.bfloat16
I32 = jnp.int32

D_MODEL = 2048
N_HEADS = 8
HEAD_DIM = 128
D_A = N_HEADS * HEAD_DIM
N_GROUPS = 8
GROUP = 128
D_B = N_GROUPS * GROUP
CHUNK = 128
GRID_W = 64
WIN_ROWS = 8
WIN_COLS = 16
CONV_W = 31
CONV_PAD = 16
N_EXPERTS = 8
EPS = 1e-6
NEG_INF = -1e30
ATTN_SCALE = HEAD_DIM ** -0.5
N_MOD_ROWS = 16
LANE = 128
D_FF_PAD = 5632

VMEM_LIMIT = 60 * 1024 * 1024

TM = 1024
TN = 1024
TM_FFN = 512
TF_FFN = 512
TF_MOE = 1024
TM_ROUTE = 512
TM_MOVE = 256
EPI_ROWS = 128
ATTN_HEAD_GROUP = 4


def _params(*sem):
    return pltpu.CompilerParams(dimension_semantics=sem, vmem_limit_bytes=VMEM_LIMIT)


def _row_loop(n_rows, fn):
    def body(c, carry):
        fn(pl.ds(pl.multiple_of(c * EPI_ROWS, EPI_ROWS), EPI_ROWS))
        return carry

    lax.fori_loop(0, n_rows // EPI_ROWS, body, 0)


def _lane_tiles(width):
    return [slice(c * LANE, (c + 1) * LANE) for c in range(width // LANE)]


def _row_mean(load, tiles):
    acc = None
    for cols in tiles:
        v = load(cols)
        acc = v if acc is None else acc + v
    return jnp.sum(acc, axis=-1, keepdims=True) / float(len(tiles) * LANE)


def _fill_modulated(x_ref, g_ref, sh_ref, sc_ref, h_ref):
    tiles = _lane_tiles(x_ref.shape[1])

    def fn(rows):
        ms = _row_mean(lambda cols: jnp.square(x_ref[rows, cols]), tiles)
        inv = lax.rsqrt(ms + EPS)
        for cols in tiles:
            y = (x_ref[rows, cols] * inv) * g_ref[:, cols]
            h_ref[rows, cols] = (y * (1.0 + sc_ref[:, cols]) + sh_ref[:, cols]).astype(h_ref.dtype)

    _row_loop(x_ref.shape[0], fn)


def _mod_spec(layer, which, row_of, width=D_MODEL, col_of=None):
    if col_of is None:
        return pl.BlockSpec((None, None, None, 1, width), lambda i, *_: (layer, row_of(i), which, 0, 0))
    return pl.BlockSpec((None, None, None, 1, width), lambda i, j, *_: (layer, row_of(i), which, 0, col_of(j)))


CAST_BLOCK_BYTES = 8 * 1024 * 1024


def _cast_kernel(w_ref, o_ref):
    o_ref[...] = w_ref[...].astype(o_ref.dtype)


def _cast_rows(rows, cols):
    best = 16
    for tr in range(16, rows + 1, 16):
        if rows % tr == 0 and tr * cols * 4 <= CAST_BLOCK_BYTES:
            best = tr
    return best


def _layer_bf16(w, layer):
    rows, cols = w.shape[-2:]
    tr = _cast_rows(rows, cols)
    if w.ndim == 3:
        grid = (rows // tr,)
        in_spec = pl.BlockSpec((None, tr, cols), lambda r: (layer, r, 0))
        out_spec = pl.BlockSpec((tr, cols), lambda r: (r, 0))
        sem = ("parallel",)
    else:
        grid = (w.shape[1], rows // tr)
        in_spec = pl.BlockSpec((None, None, tr, cols), lambda e, r: (layer, e, r, 0))
        out_spec = pl.BlockSpec((None, tr, cols), lambda e, r: (e, r, 0))
        sem = ("parallel", "parallel")
    return pl.pallas_call(
        _cast_kernel,
        grid=grid,
        in_specs=[in_spec],
        out_specs=out_spec,
        out_shape=jax.ShapeDtypeStruct(w.shape[1:], BF16),
        compiler_params=_params(*sem),
        name="weights_to_bf16",
    )(w)


def _mod_kernel(c_ref, w_ref, b_ref, o_ref):
    c = c_ref[...]
    sc = (c * jax.nn.sigmoid(c)).astype(BF16)
    o_ref[...] = jnp.dot(sc, w_ref[...].astype(BF16), preferred_element_type=F32) + b_ref[...]


def _modulation(cvec, w_mod, b_mod):
    depth, d, n = w_mod.shape
    tn = 1024
    out = pl.pallas_call(
        _mod_kernel,
        grid=(depth, n // tn),
        in_specs=[
            pl.BlockSpec((N_MOD_ROWS, d), lambda l, j: (0, 0)),
            pl.BlockSpec((None, d, tn), lambda l, j: (l, 0, j)),
            pl.BlockSpec((None, 1, tn), lambda l, j: (l, 0, j)),
        ],
        out_specs=pl.BlockSpec((None, N_MOD_ROWS, tn), lambda l, j: (l, 0, j)),
        out_shape=jax.ShapeDtypeStruct((depth, N_MOD_ROWS, n), F32),
        compiler_params=_params("parallel", "parallel"),
        name="modulation",
    )(cvec, w_mod, b_mod.reshape(depth, 1, n))
    return out.reshape(depth, N_MOD_ROWS, 6, 1, d)


def _head_rms(t, g):
    ms = jnp.mean(t * t, axis=-1, keepdims=True)
    return (t * lax.rsqrt(ms + EPS)) * g


def _even_proj_kernel(x_ref, g_ref, sh_ref, sc_ref, w_ref, qg_ref, kg_ref, lng_ref, lnb_ref, *rest, with_kv):
    if with_kv:
        p_ref, kv_ref, h_ref, acc_ref = rest
    else:
        p_ref, h_ref, acc_ref = rest
        kv_ref = None
    j = pl.program_id(1)

    @pl.when(j == 0)
    def _():
        _fill_modulated(x_ref, g_ref, sh_ref, sc_ref, h_ref)

    acc_ref[...] = jnp.dot(h_ref[...], w_ref[...], preferred_element_type=F32)
    n_rows = acc_ref.shape[0]
    tiles = _lane_tiles(acc_ref.shape[1])

    @pl.when(j == 0)
    def _():
        def fn(rows):
            for cols in tiles:
                q = _head_rms(acc_ref[rows, cols], qg_ref[...]) * ATTN_SCALE
                p_ref[rows, cols] = q.astype(BF16)
        _row_loop(n_rows, fn)

    @pl.when(j == 1)
    def _():
        def fn(rows):
            for cols in tiles:
                k = _head_rms(acc_ref[rows, cols], kg_ref[...])
                p_ref[rows, cols] = k.astype(BF16)
                if with_kv:
                    kv_ref[rows, cols] = k
        _row_loop(n_rows, fn)

    @pl.when(j == 2)
    def _():
        def fn(rows):
            for cols in tiles:
                v = acc_ref[rows, cols]
                p_ref[rows, cols] = v.astype(BF16)
                if with_kv:
                    kv_ref[rows, cols] = v
        _row_loop(n_rows, fn)

    @pl.when(j == 3)
    def _():
        def fn(rows):
            for cols in tiles:
                p_ref[rows, cols] = jax.nn.gelu(acc_ref[rows, cols]).astype(BF16)
        _row_loop(n_rows, fn)

    @pl.when(j == 4)
    def _():
        def fn(rows):
            def activated(cols):
                t = jax.nn.gelu(acc_ref[rows, cols])
                acc_ref[rows, cols] = t
                return t
            mu = _row_mean(activated, tiles)
            var = _row_mean(lambda cols: jnp.square(acc_ref[rows, cols] - mu), tiles)
            inv = lax.rsqrt(var + EPS)
            for cols in tiles:
                y = (acc_ref[rows, cols] - mu) * inv
                p_ref[rows, cols] = (y * lng_ref[:, cols] + lnb_ref[:, cols]).astype(BF16)
        _row_loop(n_rows, fn)


def _even_projection(x, mods, layer, row_of, g1, w_in, qg, kg, lng, lnb, with_kv):
    m, d = x.shape
    n = w_in.shape[1]
    out_shape = [jax.ShapeDtypeStruct((m, n), BF16)]
    out_specs = [pl.BlockSpec((TM, TN), lambda i, j: (i, j))]
    if with_kv:
        out_shape.append(jax.ShapeDtypeStruct((m, 2 * D_A), F32))
        out_specs.append(pl.BlockSpec((TM, TN), lambda i, j: (i, jnp.clip(j - 1, 0, 1))))
    vec = lambda w: pl.BlockSpec((1, w), lambda i, j: (0, 0))
    res = pl.pallas_call(
        functools.partial(_even_proj_kernel, with_kv=with_kv),
        grid=(m // TM, n // TN),
        in_specs=[
            pl.BlockSpec((TM, d), lambda i, j: (i, 0)),
            vec(d),
            _mod_spec(layer, 0, row_of),
            _mod_spec(layer, 1, row_of),
            pl.BlockSpec((d, TN), lambda i, j: (0, j)),
            vec(HEAD_DIM), vec(HEAD_DIM), vec(D_B), vec(D_B),
        ],
        out_specs=out_specs,
        out_shape=out_shape,
        scratch_shapes=[pltpu.VMEM((TM, d), BF16), pltpu.VMEM((TM, TN), F32)],
        compiler_params=_params("parallel", "arbitrary"),
        name="even_projection",
    )(x, g1, mods, mods, w_in, qg, kg, lng, lnb)
    return res if with_kv else (res[0], None)


_NT = (((1,), (1,)), ((), ()))


def _ctx_attn_kernel(q_ref, k_ref, v_ref, o_ref):
    for h in range(N_HEADS):
        cols = slice(h * HEAD_DIM, (h + 1) * HEAD_DIM)
        s = lax.dot_general(q_ref[:, cols], k_ref[:, cols], _NT, preferred_element_type=F32)
        m = jnp.max(s, axis=-1, keepdims=True)
        p = jnp.exp(s - m)
        l = jnp.sum(p, axis=-1, keepdims=True)
        o = jnp.dot(p.astype(BF16), v_ref[:, cols], preferred_element_type=F32)
        o_ref[:, cols] = (o / l).astype(o_ref.dtype)


def _context_attention(p, batch, seq):
    return pl.pallas_call(
        _ctx_attn_kernel,
        grid=(batch,),
        in_specs=[
            pl.BlockSpec((seq, D_A), lambda b: (b, 0)),
            pl.BlockSpec((seq, D_A), lambda b: (b, 1)),
            pl.BlockSpec((seq, D_A), lambda b: (b, 2)),
        ],
        out_specs=pl.BlockSpec((seq, D_A), lambda b: (b, 0)),
        out_shape=jax.ShapeDtypeStruct((batch * seq, D_A), BF16),
        compiler_params=_params("parallel"),
        name="context_attention",
    )(p, p, p)


def _bias_kernel(rpb_ref, o_ref):
    v = pl.program_id(0)
    q = lax.broadcasted_iota(I32, (GRID_W, GRID_W), 0)
    k = lax.broadcasted_iota(I32, (GRID_W, GRID_W), 1)
    dc = jnp.clip(k - q, -(WIN_COLS - 1), WIN_COLS - 1) + (WIN_COLS - 1)
    col_start = jnp.clip(q - WIN_COLS // 2, 0, GRID_W - WIN_COLS)
    col_in = (k >= col_start) & (k < col_start + WIN_COLS)
    n_dr, n_dc = 2 * WIN_ROWS - 1, 2 * WIN_COLS - 1
    for h in range(N_HEADS):
        for i in range(WIN_ROWS):
            base = (h * n_dr + v + i) * n_dc
            tile = jnp.full((GRID_W, GRID_W), NEG_INF, F32)
            for d in range(n_dc):
                tile = jnp.where(dc == d, rpb_ref[base + d], tile)
            o_ref[h, i] = jnp.where(col_in, tile, NEG_INF)


def _bias_table(rpb_e):
    tab = pl.pallas_call(
        _bias_kernel,
        grid=(WIN_ROWS,),
        in_specs=[pl.BlockSpec(memory_space=pltpu.SMEM)],
        out_specs=pl.BlockSpec((None, N_HEADS, WIN_ROWS, GRID_W, GRID_W), lambda v: (v, 0, 0, 0, 0)),
        out_shape=jax.ShapeDtypeStruct((WIN_ROWS, N_HEADS, WIN_ROWS, GRID_W, GRID_W), F32),
        compiler_params=_params("parallel"),
        name="rpb_table",
    )(rpb_e.reshape(-1))
    return tab.transpose(0, 1, 3, 2, 4).reshape(WIN_ROWS, N_HEADS, GRID_W, WIN_ROWS * GRID_W)


def _nbr_attn_kernel(q_ref, k_ref, v_ref, ck_ref, cv_ref, b_ref, o_ref, s_ref, p_ref, *, rows):
    r = pl.program_id(1)
    row_start = jnp.clip(r - WIN_ROWS // 2, 0, rows - WIN_ROWS)
    k0 = pl.multiple_of(row_start * GRID_W, GRID_W)
    n_loc = WIN_ROWS * GRID_W
    heads = [slice(h * HEAD_DIM, (h + 1) * HEAD_DIM) for h in range(N_HEADS)]
    for g0 in range(0, N_HEADS, ATTN_HEAD_GROUP):
        group = range(g0, g0 + ATTN_HEAD_GROUP)
        for h in group:
            q = q_ref[:, heads[h]]
            s_ref[h, :, :n_loc] = (
                lax.dot_general(q, k_ref[pl.ds(k0, n_loc), heads[h]], _NT, preferred_element_type=F32) + b_ref[h])
            s_ref[h, :, n_loc:] = lax.dot_general(q, ck_ref[:, heads[h]], _NT, preferred_element_type=F32)
        denom = {}
        for h in group:
            s = s_ref[h]
            p = jnp.exp(s - jnp.max(s, axis=-1, keepdims=True))
            denom[h] = jnp.sum(p, axis=-1, keepdims=True)
            p_ref[h] = p.astype(BF16)
        for h in group:
            o = (jnp.dot(p_ref[h, :, :n_loc], v_ref[pl.ds(k0, n_loc), heads[h]], preferred_element_type=F32)
                 + jnp.dot(p_ref[h, :, n_loc:], cv_ref[:, heads[h]], preferred_element_type=F32))
            o_ref[:, heads[h]] = (o / denom[h]).astype(o_ref.dtype)


def _neighbourhood_attention(p, ck, cv, bias_tab, batch, seq):
    rows = seq // GRID_W
    past = ck.shape[1]

    def placement(r):
        return jnp.clip(r - WIN_ROWS // 2, 0, rows - WIN_ROWS) - r + (WIN_ROWS - 1)

    return pl.pallas_call(
        functools.partial(_nbr_attn_kernel, rows=rows),
        grid=(batch, rows),
        in_specs=[
            pl.BlockSpec((GRID_W, D_A), lambda b, r: (b * rows + r, 0)),
            pl.BlockSpec((seq, D_A), lambda b, r: (b, 1)),
            pl.BlockSpec((seq, D_A), lambda b, r: (b, 2)),
            pl.BlockSpec((None, past, D_A), lambda b, r: (b, 0, 0)),
            pl.BlockSpec((None, past, D_A), lambda b, r: (b, 0, 0)),
            pl.BlockSpec((None, N_HEADS, GRID_W, WIN_ROWS * GRID_W), lambda b, r: (placement(r), 0, 0, 0)),
        ],
        out_specs=pl.BlockSpec((GRID_W, D_A), lambda b, r: (b * rows + r, 0)),
        out_shape=jax.ShapeDtypeStruct((batch * seq, D_A), BF16),
        scratch_shapes=[
            pltpu.VMEM((N_HEADS, GRID_W, WIN_ROWS * GRID_W + past), F32),
            pltpu.VMEM((N_HEADS, GRID_W, WIN_ROWS * GRID_W + past), BF16),
        ],
        compiler_params=_params("parallel", "arbitrary"),
        name="neighbourhood_attention",
    )(p, p, p, ck, cv, bias_tab)


def _sgu_kernel(u_ref, v_ref, w_ref, bt_ref, o_ref):
    for n in range(u_ref.shape[0] // CHUNK):
        rows = slice(n * CHUNK, (n + 1) * CHUNK)
        for g in range(N_GROUPS):
            cols = slice(g * GROUP, (g + 1) * GROUP)
            s = jnp.dot(w_ref[g], v_ref[rows, cols], preferred_element_type=F32) + bt_ref[:, g:g + 1]
            o_ref[rows, cols] = (u_ref[rows, cols].astype(F32) * s).astype(o_ref.dtype)


def _spatial_gating(p, w_s, b_s_t):
    m = p.shape[0]
    tq = 4 * CHUNK
    return pl.pallas_call(
        _sgu_kernel,
        grid=(m // tq,),
        in_specs=[
            pl.BlockSpec((tq, D_B), lambda i: (i, 3)),
            pl.BlockSpec((tq, D_B), lambda i: (i, 4)),
            pl.BlockSpec((N_GROUPS, CHUNK, CHUNK), lambda i: (0, 0, 0)),
            pl.BlockSpec((CHUNK, N_GROUPS), lambda i: (0, 0)),
        ],
        out_specs=pl.BlockSpec((tq, D_B), lambda i: (i, 0)),
        out_shape=jax.ShapeDtypeStruct((m, D_B), BF16),
        compiler_params=_params("parallel"),
        name="spatial_gating",
    )(p, p, w_s, b_s_t)


def _out_proj_kernel(x_ref, a_ref, s_ref, w_ref, gate_ref, o_ref):
    acc = jnp.dot(a_ref[...], w_ref[:D_A, :], preferred_element_type=F32)
    acc = acc + jnp.dot(s_ref[...], w_ref[D_A:, :], preferred_element_type=F32)
    o_ref[...] = x_ref[...] + gate_ref[...] * acc


def _out_projection(x, a, s, w_out, mods, layer, row_of):
    m, d = x.shape
    return pl.pallas_call(
        _out_proj_kernel,
        grid=(m // TM, d // TN),
        in_specs=[
            pl.BlockSpec((TM, TN), lambda i, j: (i, j)),
            pl.BlockSpec((TM, D_A), lambda i, j: (i, 0)),
            pl.BlockSpec((TM, D_B), lambda i, j: (i, 0)),
            pl.BlockSpec((D_A + D_B, TN), lambda i, j: (0, j)),
            _mod_spec(layer, 2, row_of, TN, lambda j: j),
        ],
        out_specs=pl.BlockSpec((TM, TN), lambda i, j: (i, j)),
        out_shape=jax.ShapeDtypeStruct((m, d), F32),
        input_output_aliases={0: 0},
        compiler_params=_params("parallel", "parallel"),
        name="even_out_projection",
    )(x, a, s, w_out, mods)


def _ffn_kernel(x_ref, g_ref, sh_ref, sc_ref, wg_ref, wu_ref, wd_ref, gate_ref, o_ref, h_ref, acc_ref):
    f = pl.program_id(1)

    @pl.when(f == 0)
    def _():
        _fill_modulated(x_ref, g_ref, sh_ref, sc_ref, h_ref)

    h = h_ref[...]
    a = jnp.dot(h, wg_ref[...], preferred_element_type=F32)
    u = jnp.dot(h, wu_ref[...], preferred_element_type=F32)
    y = jnp.dot((jax.nn.silu(a) * u).astype(BF16), wd_ref[...], preferred_element_type=F32)

    @pl.when(f == 0)
    def _():
        acc_ref[...] = y

    @pl.when(f > 0)
    def _():
        acc_ref[...] += y

    @pl.when(f == pl.num_programs(1) - 1)
    def _():
        o_ref[...] = x_ref[...] + gate_ref[...] * acc_ref[...]


def _dense_ffn(x, mods, layer, row_of, g2, wg, wu, wd):
    m, d = x.shape
    ff = wg.shape[1]
    row_tile = lambda i: row_of(i)
    return pl.pallas_call(
        _ffn_kernel,
        grid=(m // TM_FFN, ff // TF_FFN),
        in_specs=[
            pl.BlockSpec((TM_FFN, d), lambda i, f: (i, 0)),
            pl.BlockSpec((1, d), lambda i, f: (0, 0)),
            _mod_spec(layer, 3, row_tile),
            _mod_spec(layer, 4, row_tile),
            pl.BlockSpec((d, TF_FFN), lambda i, f: (0, f)),
            pl.BlockSpec((d, TF_FFN), lambda i, f: (0, f)),
            pl.BlockSpec((TF_FFN, d), lambda i, f: (f, 0)),
            _mod_spec(layer, 5, row_tile),
        ],
        out_specs=pl.BlockSpec((TM_FFN, d), lambda i, f: (i, 0)),
        out_shape=jax.ShapeDtypeStruct((m, d), F32),
        scratch_shapes=[pltpu.VMEM((TM_FFN, d), BF16), pltpu.VMEM((TM_FFN, d), F32)],
        input_output_aliases={0: 0},
        compiler_params=_params("parallel", "arbitrary"),
        name="dense_swiglu",
    )(x, g2, mods, mods, wg, wu, wd, mods)


def _glu_kernel(x_ref, g_ref, sh_ref, sc_ref, wa_ref, wg_ref, ba_ref, bg_ref, z_ref, h_ref):
    @pl.when(pl.program_id(1) == 0)
    def _():
        _fill_modulated(x_ref, g_ref, sh_ref, sc_ref, h_ref)

    h = h_ref[...]
    a = jnp.dot(h, wa_ref[...], preferred_element_type=F32) + ba_ref[...]
    gt = jnp.dot(h, wg_ref[...], preferred_element_type=F32) + bg_ref[...]
    z_ref[...] = a * jax.nn.sigmoid(gt)


def _conv_glu(x, mods, layer, row_of, g1, w1, b1):
    m, d = x.shape
    nj = d // TN
    return pl.pallas_call(
        _glu_kernel,
        grid=(m // TM, nj),
        in_specs=[
            pl.BlockSpec((TM, d), lambda i, j: (i, 0)),
            pl.BlockSpec((1, d), lambda i, j: (0, 0)),
            _mod_spec(layer, 0, row_of),
            _mod_spec(layer, 1, row_of),
            pl.BlockSpec((d, TN), lambda i, j: (0, j)),
            pl.BlockSpec((d, TN), lambda i, j: (0, j + nj)),
            pl.BlockSpec((1, TN), lambda i, j: (0, j)),
            pl.BlockSpec((1, TN), lambda i, j: (0, j + nj)),
        ],
        out_specs=pl.BlockSpec((TM, TN), lambda i, j: (i, j)),
        out_shape=jax.ShapeDtypeStruct((m, d), F32),
        scratch_shapes=[pltpu.VMEM((TM, d), BF16)],
        compiler_params=_params("parallel", "arbitrary"),
        name="conv_glu",
    )(x, g1, mods, mods, w1, w1, b1, b1)


def _dwconv_kernel(z_ref, w_ref, b_ref, o_ref, pad_ref):
    seq, tc = z_ref.shape
    rc = 64
    zeros = jnp.zeros((CONV_PAD, tc), F32)
    pad_ref[0:CONV_PAD, :] = zeros
    pad_ref[CONV_PAD + seq:, :] = zeros

    def copy(c, carry):
        r0 = pl.multiple_of(c * rc, rc)
        pad_ref[pl.ds(CONV_PAD + r0, rc), :] = z_ref[pl.ds(r0, rc), :]
        return carry

    lax.fori_loop(0, seq // rc, copy, 0)
    bias = jnp.broadcast_to(b_ref[...], (rc, tc))
    first = CONV_PAD - CONV_W // 2
    win_rows = rc + 2 * CONV_PAD
    sub = 8

    def conv(c, carry):
        r0 = pl.multiple_of(c * rc, rc)
        win = pad_ref[pl.ds(r0, win_rows), :]
        acc = bias
        for s in range(sub):
            shifted = win if s == 0 else pltpu.roll(win, win_rows - s, 0)
            for k in range(CONV_W):
                off = first + k
                if off % sub == s:
                    a0 = off - s
                    acc = acc + shifted[a0:a0 + rc, :] * w_ref[k:k + 1, :]
        o_ref[pl.ds(r0, rc), :] = acc
        return carry

    lax.fori_loop(0, seq // rc, conv, 0)


def _depthwise_conv(z, batch, seq, dw, dw_b):
    d = z.shape[1]
    tc = 256
    out = pl.pallas_call(
        _dwconv_kernel,
        grid=(batch, d // tc),
        in_specs=[
            pl.BlockSpec((None, seq, tc), lambda b, c: (b, 0, c)),
            pl.BlockSpec((CONV_W, tc), lambda b, c: (0, c)),
            pl.BlockSpec((1, tc), lambda b, c: (0, c)),
        ],
        out_specs=pl.BlockSpec((None, seq, tc), lambda b, c: (b, 0, c)),
        out_shape=jax.ShapeDtypeStruct((batch, seq, d), F32),
        scratch_shapes=[pltpu.VMEM((seq + 2 * CONV_PAD, tc), F32)],
        compiler_params=_params("parallel", "parallel"),
        name="depthwise_conv",
    )(z.reshape(batch, seq, d), dw, dw_b)
    return out.reshape(batch * seq, d)


def _conv_out_kernel(z_ref, lng_ref, lnb_ref, w_ref, b_ref, x_ref, gate_ref, o_ref, h_ref):
    @pl.when(pl.program_id(1) == 0)
    def _():
        tiles = _lane_tiles(z_ref.shape[1])

        def fn(rows):
            mu = _row_mean(lambda cols: z_ref[rows, cols], tiles)
            var = _row_mean(lambda cols: jnp.square(z_ref[rows, cols] - mu), tiles)
            inv = lax.rsqrt(var + EPS)
            for cols in tiles:
                y = ((z_ref[rows, cols] - mu) * inv) * lng_ref[:, cols] + lnb_ref[:, cols]
                h_ref[rows, cols] = jax.nn.silu(y).astype(BF16)

        _row_loop(z_ref.shape[0], fn)

    acc = jnp.dot(h_ref[...], w_ref[...], preferred_element_type=F32) + b_ref[...]
    o_ref[...] = x_ref[...] + gate_ref[...] * acc


def _conv_out(zc, x, mods, layer, row_of, ln_g, ln_b, w2, b2):
    m, d = x.shape
    return pl.pallas_call(
        _conv_out_kernel,
        grid=(m // TM, d // TN),
        in_specs=[
            pl.BlockSpec((TM, d), lambda i, j: (i, 0)),
            pl.BlockSpec((1, d), lambda i, j: (0, 0)),
            pl.BlockSpec((1, d), lambda i, j: (0, 0)),
            pl.BlockSpec((d, TN), lambda i, j: (0, j)),
            pl.BlockSpec((1, TN), lambda i, j: (0, j)),
            pl.BlockSpec((TM, TN), lambda i, j: (i, j)),
            _mod_spec(layer, 2, row_of, TN, lambda j: j),
        ],
        out_specs=pl.BlockSpec((TM, TN), lambda i, j: (i, j)),
        out_shape=jax.ShapeDtypeStruct((m, d), F32),
        scratch_shapes=[pltpu.VMEM((TM, d), BF16)],
        input_output_aliases={5: 0},
        compiler_params=_params("parallel", "arbitrary"),
        name="conv_out_projection",
    )(zc, ln_g, ln_b, w2, b2, x, mods)


def _router_kernel(x_ref, g_ref, sh_ref, sc_ref, rw_ref, base_ref, h_ref, ri_ref, rg_ref, cnt_ref, run_ref, tri_ref):
    tm = x_ref.shape[0]

    @pl.when(pl.program_id(0) == 0)
    def _():
        run_ref[...] = base_ref[...]
        src = lax.broadcasted_iota(I32, (tm, tm), 0)
        dst = lax.broadcasted_iota(I32, (tm, tm), 1)
        tri_ref[...] = jnp.where(src < dst, 1.0, 0.0).astype(BF16)

    _fill_modulated(x_ref, g_ref, sh_ref, sc_ref, h_ref)
    logits = lax.dot_general(rw_ref[...], h_ref[...], _NT, precision=lax.Precision.HIGHEST,
                             preferred_element_type=F32)
    eid = lax.broadcasted_iota(I32, logits.shape, 0).astype(F32)
    none = float(N_EXPERTS)
    m1 = jnp.max(logits, axis=0, keepdims=True)
    i1 = jnp.min(jnp.where(logits == m1, eid, none), axis=0, keepdims=True)
    rest = jnp.where(eid == i1, -jnp.inf, logits)
    m2 = jnp.max(rest, axis=0, keepdims=True)
    i2 = jnp.min(jnp.where(rest == m2, eid, none), axis=0, keepdims=True)
    e2 = jnp.exp(m2 - m1)
    gate1 = 1.0 / (1.0 + e2)
    gate2 = e2 / (1.0 + e2)
    hot1 = eid == i1
    hot2 = eid == i2
    chosen = jnp.where(hot1, 1.0, jnp.where(hot2, 1.0, 0.0))
    before = jnp.dot(chosen.astype(BF16), tri_ref[...], preferred_element_type=F32) + run_ref[:, 0:1]
    rank1 = jnp.sum(jnp.where(hot1, before, 0.0), axis=0, keepdims=True)
    rank2 = jnp.sum(jnp.where(hot2, before, 0.0), axis=0, keepdims=True)
    ri_ref[...] = jnp.zeros(ri_ref.shape, I32)
    ri_ref[0:1, :] = i1.astype(I32)
    ri_ref[1:2, :] = i2.astype(I32)
    ri_ref[2:3, :] = rank1.astype(I32)
    ri_ref[3:4, :] = rank2.astype(I32)
    rg_ref[...] = jnp.zeros(rg_ref.shape, F32)
    rg_ref[0:1, :] = gate1
    rg_ref[1:2, :] = gate2
    run_ref[...] = run_ref[...] + jnp.sum(chosen, axis=1, keepdims=True)
    cnt_ref[...] = run_ref[...]


def _router(x, mods, layer, row_of, g2, rw_t, base):
    m, d = x.shape
    tm = TM_ROUTE
    return pl.pallas_call(
        _router_kernel,
        grid=(m // tm,),
        in_specs=[
            pl.BlockSpec((tm, d), lambda i: (i, 0)),
            pl.BlockSpec((1, d), lambda i: (0, 0)),
            _mod_spec(layer, 3, row_of),
            _mod_spec(layer, 4, row_of),
            pl.BlockSpec((N_EXPERTS, d), lambda i: (0, 0)),
            pl.BlockSpec((N_EXPERTS, LANE), lambda i: (0, 0)),
        ],
        out_specs=[
            pl.BlockSpec((tm, d), lambda i: (i, 0)),
            pl.BlockSpec((8, tm), lambda i: (0, i)),
            pl.BlockSpec((8, tm), lambda i: (0, i)),
            pl.BlockSpec((N_EXPERTS, LANE), lambda i: (0, 0)),
        ],
        out_shape=[
            jax.ShapeDtypeStruct((m, d), F32),
            jax.ShapeDtypeStruct((8, m), I32),
            jax.ShapeDtypeStruct((8, m), F32),
            jax.ShapeDtypeStruct((N_EXPERTS, LANE), F32),
        ],
        scratch_shapes=[pltpu.VMEM((N_EXPERTS, LANE), F32), pltpu.VMEM((tm, tm), BF16)],
        compiler_params=_params("arbitrary"),
        name="moe_router",
    )(x, g2, mods, mods, rw_t, base)


def _dispatch_kernel(pos_ref, h_ref, xs_in_ref, xs_ref, sem):
    del xs_in_ref
    tm = h_ref.shape[0]
    n_tok = pos_ref.shape[0] // 2
    t0 = pl.program_id(0) * tm

    def row_copy(t, k):
        dst = pos_ref[k * n_tok + t0 + t]
        return pltpu.make_async_copy(h_ref.at[pl.ds(t, 1)], xs_ref.at[pl.ds(dst, 1)], sem)

    def start(t, carry):
        row_copy(t, 0).start()
        row_copy(t, 1).start()
        return carry

    def wait(t, carry):
        row_copy(t, 0).wait()
        row_copy(t, 1).wait()
        return carry

    lax.fori_loop(0, tm, start, 0, unroll=4)
    lax.fori_loop(0, tm, wait, 0, unroll=4)


def _dispatch(pos_flat, h, xs):
    m, d = h.shape
    return pl.pallas_call(
        _dispatch_kernel,
        grid_spec=pltpu.PrefetchScalarGridSpec(
            num_scalar_prefetch=1,
            grid=(m // TM_MOVE,),
            in_specs=[
                pl.BlockSpec((TM_MOVE, d), lambda i, pos: (i, 0)),
                pl.BlockSpec(memory_space=pl.ANY),
            ],
            out_specs=pl.BlockSpec(memory_space=pl.ANY),
            scratch_shapes=[pltpu.SemaphoreType.DMA(())],
        ),
        out_shape=jax.ShapeDtypeStruct(xs.shape, xs.dtype),
        input_output_aliases={2: 0},
        compiler_params=_params("arbitrary"),
        name="moe_dispatch",
    )(pos_flat, h, xs)


def _expert_kernel(te_ref, nu_ref, xs_ref, wg_ref, wu_ref, wd_ref, ys_ref, xb_ref, acc_ref):
    t = pl.program_id(0)
    f = pl.program_id(1)
    last = pl.num_programs(1) - 1
    used = t < nu_ref[0]

    @pl.when(used & (f == 0))
    def _():
        xb_ref[...] = xs_ref[...].astype(BF16)

    @pl.when(used)
    def _():
        h = xb_ref[...]
        a = jnp.dot(h, wg_ref[...], preferred_element_type=F32)
        u = jnp.dot(h, wu_ref[...], preferred_element_type=F32)
        y = jnp.dot((jax.nn.silu(a) * u).astype(BF16), wd_ref[...], preferred_element_type=F32)

        @pl.when(f == 0)
        def _():
            acc_ref[...] = y

        @pl.when(f > 0)
        def _():
            acc_ref[...] += y

    @pl.when(used & (f == last))
    def _():
        ys_ref[...] = acc_ref[...]

    @pl.when(jnp.logical_not(used) & (f == last))
    def _():
        ys_ref[...] = jnp.zeros(ys_ref.shape, ys_ref.dtype)


def _experts(tile_expert, n_used, xs, wg, wu, wd):
    p, d = xs.shape
    ff = wg.shape[2]
    nf = ff // TF_MOE

    def row_idx(t, f, te, nu):
        return (jnp.minimum(t, nu[0] - 1), 0)

    def hid(t, f, nu):
        return jnp.where(t < nu[0], f, nf - 1)

    return pl.pallas_call(
        _expert_kernel,
        grid_spec=pltpu.PrefetchScalarGridSpec(
            num_scalar_prefetch=2,
            grid=(p // TM_FFN, nf),
            in_specs=[
                pl.BlockSpec((TM_FFN, d), row_idx),
                pl.BlockSpec((None, d, TF_MOE), lambda t, f, te, nu: (te[t], 0, hid(t, f, nu))),
                pl.BlockSpec((None, d, TF_MOE), lambda t, f, te, nu: (te[t], 0, hid(t, f, nu))),
                pl.BlockSpec((None, TF_MOE, d), lambda t, f, te, nu: (te[t], hid(t, f, nu), 0)),
            ],
            out_specs=pl.BlockSpec((TM_FFN, d), lambda t, f, te, nu: (t, 0)),
            scratch_shapes=[pltpu.VMEM((TM_FFN, d), BF16), pltpu.VMEM((TM_FFN, d), F32)],
        ),
        out_shape=jax.ShapeDtypeStruct((p, d), F32),
        compiler_params=_params("arbitrary", "arbitrary"),
        name="moe_experts",
    )(tile_expert, n_used, xs, wg, wu, wd)


def _combine_kernel(pos_ref, ys_ref, x_ref, gt_ref, gate_ref, o_ref, buf_ref, sem):
    tm = x_ref.shape[0]
    n_tok = pos_ref.shape[0] // 2
    t0 = pl.program_id(0) * tm

    def row_copy(t, k):
        src = pos_ref[k * n_tok + t0 + t]
        return pltpu.make_async_copy(ys_ref.at[pl.ds(src, 1)], buf_ref.at[k, pl.ds(t, 1)], sem)

    def start(t, carry):
        row_copy(t, 0).start()
        row_copy(t, 1).start()
        return carry

    def wait(t, carry):
        row_copy(t, 0).wait()
        row_copy(t, 1).wait()
        return carry

    lax.fori_loop(0, tm, start, 0, unroll=4)
    lax.fori_loop(0, tm, wait, 0, unroll=4)
    g1 = gt_ref[:, 0:1]
    g2 = gt_ref[:, 1:2]
    for cols in _lane_tiles(x_ref.shape[1]):
        y = g1 * buf_ref[0, :, cols] + g2 * buf_ref[1, :, cols]
        o_ref[:, cols] = x_ref[:, cols] + gate_ref[:, cols] * y


def _combine(pos_flat, ys, x, gates_t, mods, layer, row_of):
    m, d = x.shape
    return pl.pallas_call(
        _combine_kernel,
        grid_spec=pltpu.PrefetchScalarGridSpec(
            num_scalar_prefetch=1,
            grid=(m // TM_MOVE,),
            in_specs=[
                pl.BlockSpec(memory_space=pl.ANY),
                pl.BlockSpec((TM_MOVE, d), lambda i, pos: (i, 0)),
                pl.BlockSpec((TM_MOVE, 2), lambda i, pos: (i, 0)),
                _mod_spec(layer, 5, row_of),
            ],
            out_specs=pl.BlockSpec((TM_MOVE, d), lambda i, pos: (i, 0)),
            scratch_shapes=[pltpu.VMEM((2, TM_MOVE, d), F32), pltpu.SemaphoreType.DMA(())],
        ),
        out_shape=jax.ShapeDtypeStruct((m, d), F32),
        input_output_aliases={2: 0},
        compiler_params=_params("arbitrary"),
        name="moe_combine",
    )(pos_flat, ys, x, gates_t, mods)


def _moe(streams, mods, layer, g2, rw_t, wg, wu, wd):
    n_rows = sum(2 * x.shape[0] for x, _, _ in streams)
    n_tiles = n_rows // TM_FFN + N_EXPERTS
    base = jnp.zeros((N_EXPERTS, LANE), F32)
    routed = []
    for x, row_of, tile in streams:
        h, ri, rg, base = _router(x, mods, layer, lambda i, ro=row_of: ro(i, TM_ROUTE), g2, rw_t, base)
        routed.append((h, ri, rg))
    counts = base[:, 0].astype(I32)
    tiles_e = (counts + TM_FFN - 1) // TM_FFN
    tile_end = jnp.cumsum(tiles_e)
    row_off = (tile_end - tiles_e) * TM_FFN
    n_used = tile_end[-1:]
    tile_ids = jnp.minimum(jnp.arange(n_tiles, dtype=I32), n_used[0] - 1)
    tile_expert = jnp.sum(tile_ids[:, None] >= tile_end[None, :], axis=1).astype(I32)
    xs = jnp.zeros((n_tiles * TM_FFN, D_MODEL), F32)
    pos_all = []
    for h, ri, rg in routed:
        ids, rank = ri[0:2], ri[2:4]
        off = jnp.sum(jnp.where(ids[:, :, None] == jnp.arange(N_EXPERTS)[None, None, :], row_off[None, None, :], 0), axis=-1)
        pos = (off + rank).reshape(-1)
        pos_all.append(pos)
        xs = _dispatch(pos, h, xs)
    ys = _experts(tile_expert, n_used, xs, wg, wu, wd)
    outs = []
    for (x, row_of, tile), (h, ri, rg), pos in zip(streams, routed, pos_all):
        outs.append(_combine(pos, ys, x, rg[0:2].T, mods, layer, lambda i, ro=row_of: ro(i, TM_MOVE)))
    return outs


def kernel(x_prompt, x_sample, cache_k, cache_v, c, c_ctx, w_mod, b_mod, norm1_g, norm2_g, w_in_even, q_norm_g, k_norm_g, rpb, sgu_ln_g, sgu_ln_b, sgu_w, sgu_b, w_out_even, ffn_w_gate, ffn_w_up, ffn_w_down, conv_w1, conv_b1, conv_dw, conv_dw_b, conv_ln_g, conv_ln_b, conv_w2, conv_b2, router_w, moe_w_gate, moe_w_up, moe_w_down):
    batch, seq, d = x_prompt.shape
    dec_batch, dec_seq, _ = x_sample.shape
    depth = w_mod.shape[0]
    n_even = w_in_even.shape[0]
    past = cache_k.shape[2]
    xp = x_prompt.reshape(batch * seq, d)
    xs = x_sample.reshape(dec_batch * dec_seq, d)

    cvec = jnp.zeros((N_MOD_ROWS, d), F32).at[0].set(c_ctx).at[1:1 + dec_batch].set(c)
    mods = _modulation(cvec, w_mod, b_mod)

    prompt_row = lambda i, tile=TM: 0
    sample_row = lambda i, tile=TM: 1 + (i * tile) // dec_seq
    row2 = lambda a: a.reshape(1, -1)

    new_k, new_v = [], []
    for l in range(depth):
        g1, g2 = row2(norm1_g[l]), row2(norm2_g[l])
        if l % 2 == 0:
            e = l // 2
            w_in = _layer_bf16(w_in_even, e)
            w_out = _layer_bf16(w_out_even, e)
            w_s = sgu_w[e].astype(BF16)
            b_s_t = sgu_b[e].T
            ff_pad = D_FF_PAD - ffn_w_gate.shape[2]
            wg = jnp.pad(_layer_bf16(ffn_w_gate, e), ((0, 0), (0, ff_pad)))
            wu = jnp.pad(_layer_bf16(ffn_w_up, e), ((0, 0), (0, ff_pad)))
            wd = jnp.pad(_layer_bf16(ffn_w_down, e), ((0, ff_pad), (0, 0)))
            ck = cache_k[:, e].reshape(dec_batch, past, D_A).astype(BF16)
            cv = cache_v[:, e].reshape(dec_batch, past, D_A).astype(BF16)
            bias_tab = _bias_table(rpb[e])
            norm_args = (row2(q_norm_g[e]), row2(k_norm_g[e]), row2(sgu_ln_g[e]), row2(sgu_ln_b[e]))

            pp, kv = _even_projection(xp, mods, l, prompt_row, g1, w_in, *norm_args, with_kv=True)
            new_k.append(kv[:, :D_A].reshape(batch, seq, N_HEADS, HEAD_DIM))
            new_v.append(kv[:, D_A:].reshape(batch, seq, N_HEADS, HEAD_DIM))
            ap = _context_attention(pp, batch, seq)
            sp = _spatial_gating(pp, w_s, b_s_t)
            xp = _out_projection(xp, ap, sp, w_out, mods, l, prompt_row)

            ps, _ = _even_projection(xs, mods, l, sample_row, g1, w_in, *norm_args, with_kv=False)
            a_s = _neighbourhood_attention(ps, ck, cv, bias_tab, dec_batch, dec_seq)
            s_s = _spatial_gating(ps, w_s, b_s_t)
            xs = _out_projection(xs, a_s, s_s, w_out, mods, l, sample_row)

            xp = _dense_ffn(xp, mods, l, lambda i: prompt_row(i, TM_FFN), g2, wg, wu, wd)
            xs = _dense_ffn(xs, mods, l, lambda i: sample_row(i, TM_FFN), g2, wg, wu, wd)
        else:
            o = l // 2
            w1 = _layer_bf16(conv_w1, o)
            w2 = _layer_bf16(conv_w2, o)
            b1, b2 = row2(conv_b1[o]), row2(conv_b2[o])
            dw, dw_b = conv_dw[o], row2(conv_dw_b[o])
            ln_g, ln_b = row2(conv_ln_g[o]), row2(conv_ln_b[o])
            outs = []
            for x, row_of, nb, ns in ((xp, prompt_row, batch, seq), (xs, sample_row, dec_batch, dec_seq)):
                z = _conv_glu(x, mods, l, row_of, g1, w1, b1)
                zc = _depthwise_conv(z, nb, ns, dw, dw_b)
                outs.append(_conv_out(zc, x, mods, l, row_of, ln_g, ln_b, w2, b2))
            xp, xs = outs
            xp, xs = _moe(
                [(xp, prompt_row, None), (xs, sample_row, None)], mods, l, g2,
                router_w[o].T, _layer_bf16(moe_w_gate, o), _layer_bf16(moe_w_up, o), _layer_bf16(moe_w_down, o))

    new_cache_k = jnp.stack(new_k, axis=1)
    new_cache_v = jnp.stack(new_v, axis=1)
    return (xp.reshape(batch, seq, d), xs.reshape(dec_batch, dec_seq, d), new_cache_k, new_cache_v)
```

```python
import functools

import jax
import jax.numpy as jnp
from jax import lax
from jax.experimental import pallas as pl
from jax.experimental.pallas import tpu as pltpu

F32 = jnp.float32
BF16 = jnp.bfloat16
I32 = jnp.int32

D_MODEL = 2048
N_HEADS = 8
HEAD_DIM = 128
D_A = N_HEADS * HEAD_DIM
N_GROUPS = 8
GROUP = 128
D_B = N_GROUPS * GROUP
CHUNK = 128
GRID_W = 64
WIN_ROWS = 8
WIN_COLS = 16
CONV_W = 31
CONV_PAD = 16
N_EXPERTS = 8
EPS = 1e-6
NEG_INF = -1e30
ATTN_SCALE = HEAD_DIM ** -0.5
N_MOD_ROWS = 16
LANE = 128
D_FF_PAD = 5632

VMEM_LIMIT = 60 * 1024 * 1024

TM = 1024
TN = 1024
TM_FFN = 512
TF_FFN = 512
TF_MOE = 1024
TM_ROUTE = 512
TM_MOVE = 256
EPI_ROWS = 128
ATTN_HEAD_GROUP = 4
CAST_UNITS = 256


def _params(*sem):
    return pltpu.CompilerParams(dimension_semantics=sem, vmem_limit_bytes=VMEM_LIMIT)


def _row_loop(n_rows, fn):
    def body(c, carry):
        fn(pl.ds(pl.multiple_of(c * EPI_ROWS, EPI_ROWS), EPI_ROWS))
        return carry

    lax.fori_loop(0, n_rows // EPI_ROWS, body, 0)


def _lane_tiles(width):
    return [slice(c * LANE, (c + 1) * LANE) for c in range(width // LANE)]


def _row_mean(load, tiles):
    acc = None
    for cols in tiles:
        v = load(cols)
        acc = v if acc is None else acc + v
    return jnp.sum(acc, axis=-1, keepdims=True) / float(len(tiles) * LANE)


def _fill_modulated(x_ref, g_ref, sh_ref, sc_ref, h_ref):
    tiles = _lane_tiles(x_ref.shape[1])

    def fn(rows):
        ms = _row_mean(lambda cols: jnp.square(x_ref[rows, cols]), tiles)
        inv = lax.rsqrt(ms + EPS)
        for cols in tiles:
            y = (x_ref[rows, cols] * inv) * g_ref[:, cols]
            h_ref[rows, cols] = (y * (1.0 + sc_ref[:, cols]) + sh_ref[:, cols]).astype(h_ref.dtype)

    _row_loop(x_ref.shape[0], fn)


def _mod_spec(layer, which, row_of, width=D_MODEL, col_of=None):
    if col_of is None:
        return pl.BlockSpec((None, None, None, 1, width), lambda i, *_: (layer, row_of(i), which, 0, 0))
    return pl.BlockSpec((None, None, None, 1, width), lambda i, j, *_: (layer, row_of(i), which, 0, col_of(j)))


CAST_BLOCK_BYTES = 8 * 1024 * 1024


def _cast_kernel(w_ref, o_ref):
    o_ref[...] = w_ref[...].astype(o_ref.dtype)


def _cast_rows(rows, cols):
    best = 16
    for tr in range(16, rows + 1, 16):
        if rows % tr == 0 and tr * cols * 4 <= CAST_BLOCK_BYTES:
            best = tr
    return best


def _layer_bf16(w, layer):
    rows, cols = w.shape[-2:]
    tr = _cast_rows(rows, cols)
    if w.ndim == 3:
        grid = (rows // tr,)
        in_spec = pl.BlockSpec((None, tr, cols), lambda r: (layer, r, 0))
        out_spec = pl.BlockSpec((tr, cols), lambda r: (r, 0))
        sem = ("parallel",)
    else:
        grid = (w.shape[1], rows // tr)
        in_spec = pl.BlockSpec((None, None, tr, cols), lambda e, r: (layer, e, r, 0))
        out_spec = pl.BlockSpec((None, tr, cols), lambda e, r: (e, r, 0))
        sem = ("parallel", "parallel")
    return pl.pallas_call(
        _cast_kernel,
        grid=grid,
        in_specs=[in_spec],
        out_specs=out_spec,
        out_shape=jax.ShapeDtypeStruct(w.shape[1:], BF16),
        compiler_params=_params(*sem),
        name="weights_to_bf16",
    )(w)


def _mod_kernel(c_ref, w_ref, b_ref, o_ref):
    c = c_ref[...]
    sc = (c * jax.nn.sigmoid(c)).astype(BF16)
    o_ref[...] = jnp.dot(sc, w_ref[...].astype(BF16), preferred_element_type=F32) + b_ref[...]


def _modulation(cvec, w_mod, b_mod):
    depth, d, n = w_mod.shape
    tn = 1024
    out = pl.pallas_call(
        _mod_kernel,
        grid=(depth, n // tn),
        in_specs=[
            pl.BlockSpec((N_MOD_ROWS, d), lambda l, j: (0, 0)),
            pl.BlockSpec((None, d, tn), lambda l, j: (l, 0, j)),
            pl.BlockSpec((None, 1, tn), lambda l, j: (l, 0, j)),
        ],
        out_specs=pl.BlockSpec((None, N_MOD_ROWS, tn), lambda l, j: (l, 0, j)),
        out_shape=jax.ShapeDtypeStruct((depth, N_MOD_ROWS, n), F32),
        compiler_params=_params("parallel", "parallel"),
        name="modulation",
    )(cvec, w_mod, b_mod.reshape(depth, 1, n))
    return out.reshape(depth, N_MOD_ROWS, 6, 1, d)


def _head_rms(t, g):
    ms = jnp.mean(t * t, axis=-1, keepdims=True)
    return (t * lax.rsqrt(ms + EPS)) * g


def _even_proj_kernel(x_ref, g_ref, sh_ref, sc_ref, w_ref, qg_ref, kg_ref, lng_ref, lnb_ref, *rest, with_kv):
    if with_kv:
        p_ref, kv_ref, h_ref, acc_ref = rest
    else:
        p_ref, h_ref, acc_ref = rest
        kv_ref = None
    j = pl.program_id(1)

    @pl.when(j == 0)
    def _():
        _fill_modulated(x_ref, g_ref, sh_ref, sc_ref, h_ref)

    acc_ref[...] = jnp.dot(h_ref[...], w_ref[...], preferred_element_type=F32)
    n_rows = acc_ref.shape[0]
    tiles = _lane_tiles(acc_ref.shape[1])

    @pl.when(j == 0)
    def _():
        def fn(rows):
            for cols in tiles:
                q = _head_rms(acc_ref[rows, cols], qg_ref[...]) * ATTN_SCALE
                p_ref[rows, cols] = q.astype(BF16)
        _row_loop(n_rows, fn)

    @pl.when(j == 1)
    def _():
        def fn(rows):
            for cols in tiles:
                k = _head_rms(acc_ref[rows, cols], kg_ref[...])
                p_ref[rows, cols] = k.astype(BF16)
                if with_kv:
                    kv_ref[rows, cols] = k
        _row_loop(n_rows, fn)

    @pl.when(j == 2)
    def _():
        def fn(rows):
            for cols in tiles:
                v = acc_ref[rows, cols]
                p_ref[rows, cols] = v.astype(BF16)
                if with_kv:
                    kv_ref[rows, cols] = v
        _row_loop(n_rows, fn)

    @pl.when(j == 3)
    def _():
        def fn(rows):
            for cols in tiles:
                p_ref[rows, cols] = jax.nn.gelu(acc_ref[rows, cols]).astype(BF16)
        _row_loop(n_rows, fn)

    @pl.when(j == 4)
    def _():
        def fn(rows):
            def activated(cols):
                t = jax.nn.gelu(acc_ref[rows, cols])
                acc_ref[rows, cols] = t
                return t
            mu = _row_mean(activated, tiles)
            var = _row_mean(lambda cols: jnp.square(acc_ref[rows, cols] - mu), tiles)
            inv = lax.rsqrt(var + EPS)
            for cols in tiles:
                y = (acc_ref[rows, cols] - mu) * inv
                p_ref[rows, cols] = (y * lng_ref[:, cols] + lnb_ref[:, cols]).astype(BF16)
        _row_loop(n_rows, fn)


def _even_projection(x, mods, layer, row_of, g1, w_in, qg, kg, lng, lnb, with_kv):
    m, d = x.shape
    n = w_in.shape[1]
    out_shape = [jax.ShapeDtypeStruct((m, n), BF16)]
    out_specs = [pl.BlockSpec((TM, TN), lambda i, j: (i, j))]
    if with_kv:
        out_shape.append(jax.ShapeDtypeStruct((m, 2 * D_A), F32))
        out_specs.append(pl.BlockSpec((TM, TN), lambda i, j: (i, jnp.clip(j - 1, 0, 1))))
    vec = lambda w: pl.BlockSpec((1, w), lambda i, j: (0, 0))
    res = pl.pallas_call(
        functools.partial(_even_proj_kernel, with_kv=with_kv),
        grid=(m // TM, n // TN),
        in_specs=[
            pl.BlockSpec((TM, d), lambda i, j: (i, 0)),
            vec(d),
            _mod_spec(layer, 0, row_of),
            _mod_spec(layer, 1, row_of),
            pl.BlockSpec((d, TN), lambda i, j: (0, j)),
            vec(HEAD_DIM), vec(HEAD_DIM), vec(D_B), vec(D_B),
        ],
        out_specs=out_specs,
        out_shape=out_shape,
        scratch_shapes=[pltpu.VMEM((TM, d), BF16), pltpu.VMEM((TM, TN), F32)],
        compiler_params=_params("parallel", "arbitrary"),
        name="even_projection",
    )(x, g1, mods, mods, w_in, qg, kg, lng, lnb)
    return res if with_kv else (res[0], None)


_NT = (((1,), (1,)), ((), ()))


def _ctx_attn_kernel(q_ref, k_ref, v_ref, o_ref):
    for h in range(N_HEADS):
        cols = slice(h * HEAD_DIM, (h + 1) * HEAD_DIM)
        s = lax.dot_general(q_ref[:, cols], k_ref[:, cols], _NT, preferred_element_type=F32)
        m = jnp.max(s, axis=-1, keepdims=True)
        p = jnp.exp(s - m)
        l = jnp.sum(p, axis=-1, keepdims=True)
        o = jnp.dot(p.astype(BF16), v_ref[:, cols], preferred_element_type=F32)
        o_ref[:, cols] = (o / l).astype(o_ref.dtype)


def _context_attention(p, batch, seq):
    return pl.pallas_call(
        _ctx_attn_kernel,
        grid=(batch,),
        in_specs=[
            pl.BlockSpec((seq, D_A), lambda b: (b, 0)),
            pl.BlockSpec((seq, D_A), lambda b: (b, 1)),
            pl.BlockSpec((seq, D_A), lambda b: (b, 2)),
        ],
        out_specs=pl.BlockSpec((seq, D_A), lambda b: (b, 0)),
        out_shape=jax.ShapeDtypeStruct((batch * seq, D_A), BF16),
        compiler_params=_params("parallel"),
        name="context_attention",
    )(p, p, p)


def _bias_kernel(rpb_ref, o_ref):
    v = pl.program_id(0)
    q = lax.broadcasted_iota(I32, (GRID_W, GRID_W), 0)
    k = lax.broadcasted_iota(I32, (GRID_W, GRID_W), 1)
    dc = jnp.clip(k - q, -(WIN_COLS - 1), WIN_COLS - 1) + (WIN_COLS - 1)
    col_start = jnp.clip(q - WIN_COLS // 2, 0, GRID_W - WIN_COLS)
    col_in = (k >= col_start) & (k < col_start + WIN_COLS)
    n_dr, n_dc = 2 * WIN_ROWS - 1, 2 * WIN_COLS - 1
    for h in range(N_HEADS):
        for i in range(WIN_ROWS):
            base = (h * n_dr + v + i) * n_dc
            tile = jnp.full((GRID_W, GRID_W), NEG_INF, F32)
            for d in range(n_dc):
                tile = jnp.where(dc == d, rpb_ref[base + d], tile)
            o_ref[h, i] = jnp.where(col_in, tile, NEG_INF)


def _bias_table(rpb_e):
    tab = pl.pallas_call(
        _bias_kernel,
        grid=(WIN_ROWS,),
        in_specs=[pl.BlockSpec(memory_space=pltpu.SMEM)],
        out_specs=pl.BlockSpec((None, N_HEADS, WIN_ROWS, GRID_W, GRID_W), lambda v: (v, 0, 0, 0, 0)),
        out_shape=jax.ShapeDtypeStruct((WIN_ROWS, N_HEADS, WIN_ROWS, GRID_W, GRID_W), F32),
        compiler_params=_params("parallel"),
        name="rpb_table",
    )(rpb_e.reshape(-1))
    return tab.transpose(0, 1, 3, 2, 4).reshape(WIN_ROWS, N_HEADS, GRID_W, WIN_ROWS * GRID_W)


def _nbr_attn_kernel(q_ref, k_ref, v_ref, ck_ref, cv_ref, b_ref, o_ref, s_ref, p_ref, *, rows):
    r = pl.program_id(1)
    row_start = jnp.clip(r - WIN_ROWS // 2, 0, rows - WIN_ROWS)
    k0 = pl.multiple_of(row_start * GRID_W, GRID_W)
    n_loc = WIN_ROWS * GRID_W
    heads = [slice(h * HEAD_DIM, (h + 1) * HEAD_DIM) for h in range(N_HEADS)]
    for g0 in range(0, N_HEADS, ATTN_HEAD_GROUP):
        group = range(g0, g0 + ATTN_HEAD_GROUP)
        for h in group:
            q = q_ref[:, heads[h]]
            s_ref[h, :, :n_loc] = (
                lax.dot_general(q, k_ref[pl.ds(k0, n_loc), heads[h]], _NT, preferred_element_type=F32) + b_ref[h])
            s_ref[h, :, n_loc:] = lax.dot_general(q, ck_ref[:, heads[h]], _NT, preferred_element_type=F32)
        denom = {}
        for h in group:
            s = s_ref[h]
            p = jnp.exp(s - jnp.max(s, axis=-1, keepdims=True))
            denom[h] = jnp.sum(p, axis=-1, keepdims=True)
            p_ref[h] = p.astype(BF16)
        for h in group:
            o = (jnp.dot(p_ref[h, :, :n_loc], v_ref[pl.ds(k0, n_loc), heads[h]], preferred_element_type=F32)
                 + jnp.dot(p_ref[h, :, n_loc:], cv_ref[:, heads[h]], preferred_element_type=F32))
            o_ref[:, heads[h]] = (o / denom[h]).astype(o_ref.dtype)


def _neighbourhood_attention(p, ck, cv, bias_tab, batch, seq):
    rows = seq // GRID_W
    past = ck.shape[1]

    def placement(r):
        return jnp.clip(r - WIN_ROWS // 2, 0, rows - WIN_ROWS) - r + (WIN_ROWS - 1)

    return pl.pallas_call(
        functools.partial(_nbr_attn_kernel, rows=rows),
        grid=(batch, rows),
        in_specs=[
            pl.BlockSpec((GRID_W, D_A), lambda b, r: (b * rows + r, 0)),
            pl.BlockSpec((seq, D_A), lambda b, r: (b, 1)),
            pl.BlockSpec((seq, D_A), lambda b, r: (b, 2)),
            pl.BlockSpec((None, past, D_A), lambda b, r: (b, 0, 0)),
            pl.BlockSpec((None, past, D_A), lambda b, r: (b, 0, 0)),
            pl.BlockSpec((None, N_HEADS, GRID_W, WIN_ROWS * GRID_W), lambda b, r: (placement(r), 0, 0, 0)),
        ],
        out_specs=pl.BlockSpec((GRID_W, D_A), lambda b, r: (b * rows + r, 0)),
        out_shape=jax.ShapeDtypeStruct((batch * seq, D_A), BF16),
        scratch_shapes=[
            pltpu.VMEM((N_HEADS, GRID_W, WIN_ROWS * GRID_W + past), F32),
            pltpu.VMEM((N_HEADS, GRID_W, WIN_ROWS * GRID_W + past), BF16),
        ],
        compiler_params=_params("parallel", "arbitrary"),
        name="neighbourhood_attention",
    )(p, p, p, ck, cv, bias_tab)


def _sgu_kernel(u_ref, v_ref, w_ref, bt_ref, o_ref):
    for n in range(u_ref.shape[0] // CHUNK):
        rows = slice(n * CHUNK, (n + 1) * CHUNK)
        for g in range(N_GROUPS):
            cols = slice(g * GROUP, (g + 1) * GROUP)
            s = jnp.dot(w_ref[g], v_ref[rows, cols], preferred_element_type=F32) + bt_ref[:, g:g + 1]
            o_ref[rows, cols] = (u_ref[rows, cols].astype(F32) * s).astype(o_ref.dtype)


def _spatial_gating(p, w_s, b_s_t):
    m = p.shape[0]
    tq = 4 * CHUNK
    return pl.pallas_call(
        _sgu_kernel,
        grid=(m // tq,),
        in_specs=[
            pl.BlockSpec((tq, D_B), lambda i: (i, 3)),
            pl.BlockSpec((tq, D_B), lambda i: (i, 4)),
            pl.BlockSpec((N_GROUPS, CHUNK, CHUNK), lambda i: (0, 0, 0)),
            pl.BlockSpec((CHUNK, N_GROUPS), lambda i: (0, 0)),
        ],
        out_specs=pl.BlockSpec((tq, D_B), lambda i: (i, 0)),
        out_shape=jax.ShapeDtypeStruct((m, D_B), BF16),
        compiler_params=_params("parallel"),
        name="spatial_gating",
    )(p, p, w_s, b_s_t)


def _out_proj_kernel(x_ref, a_ref, s_ref, w_ref, gate_ref, o_ref):
    acc = jnp.dot(a_ref[...], w_ref[:D_A, :], preferred_element_type=F32)
    acc = acc + jnp.dot(s_ref[...], w_ref[D_A:, :], preferred_element_type=F32)
    o_ref[...] = x_ref[...] + gate_ref[...] * acc


def _out_projection(x, a, s, w_out, mods, layer, row_of):
    m, d = x.shape
    return pl.pallas_call(
        _out_proj_kernel,
        grid=(m // TM, d // TN),
        in_specs=[
            pl.BlockSpec((TM, TN), lambda i, j: (i, j)),
            pl.BlockSpec((TM, D_A), lambda i, j: (i, 0)),
            pl.BlockSpec((TM, D_B), lambda i, j: (i, 0)),
            pl.BlockSpec((D_A + D_B, TN), lambda i, j: (0, j)),
            _mod_spec(layer, 2, row_of, TN, lambda j: j),
        ],
        out_specs=pl.BlockSpec((TM, TN), lambda i, j: (i, j)),
        out_shape=jax.ShapeDtypeStruct((m, d), F32),
        input_output_aliases={0: 0},
        compiler_params=_params("parallel", "parallel"),
        name="even_out_projection",
    )(x, a, s, w_out, mods)


def _ffn_kernel(x_ref, g_ref, sh_ref, sc_ref, wg_ref, wu_ref, wd_ref, gate_ref, *rest, n_cast):
    cast_in = rest[:n_cast]
    o_ref = rest[n_cast]
    cast_out = rest[n_cast + 1:2 * n_cast + 1]
    h_ref, acc_ref = rest[2 * n_cast + 1:]
    f = pl.program_id(1)

    @pl.when(f == 0)
    def _():
        _fill_modulated(x_ref, g_ref, sh_ref, sc_ref, h_ref)

    h = h_ref[...]
    a = jnp.dot(h, wg_ref[...], preferred_element_type=F32)
    u = jnp.dot(h, wu_ref[...], preferred_element_type=F32)
    y = jnp.dot((jax.nn.silu(a) * u).astype(BF16), wd_ref[...], preferred_element_type=F32)

    @pl.when(f == 0)
    def _():
        acc_ref[...] = y

    @pl.when(f > 0)
    def _():
        acc_ref[...] += y

    @pl.when(f == pl.num_programs(1) - 1)
    def _():
        o_ref[...] = x_ref[...] + gate_ref[...] * acc_ref[...]

    if n_cast:
        @pl.when(pl.program_id(0) * pl.num_programs(1) + f < CAST_UNITS)
        def _():
            for src, dst in zip(cast_in, cast_out):
                dst[...] = src[...].astype(dst.dtype)


def _dense_ffn(x, mods, layer, row_of, g2, wg, wu, wd, cast=None):
    m, d = x.shape
    ff = wg.shape[1]
    nf = ff // TF_FFN
    row_tile = lambda i: row_of(i)
    in_specs = [
        pl.BlockSpec((TM_FFN, d), lambda i, f: (i, 0)),
        pl.BlockSpec((1, d), lambda i, f: (0, 0)),
        _mod_spec(layer, 3, row_tile),
        _mod_spec(layer, 4, row_tile),
        pl.BlockSpec((d, TF_FFN), lambda i, f: (0, f)),
        pl.BlockSpec((d, TF_FFN), lambda i, f: (0, f)),
        pl.BlockSpec((TF_FFN, d), lambda i, f: (f, 0)),
        _mod_spec(layer, 5, row_tile),
    ]
    out_specs = [pl.BlockSpec((TM_FFN, d), lambda i, f: (i, 0))]
    out_shape = [jax.ShapeDtypeStruct((m, d), F32)]
    cast_args = []
    if cast is not None:
        tensors, cast_layer = cast
        assert (m // TM_FFN) * nf >= CAST_UNITS
        for w in tensors:
            _, n_exp, rows, cols = w.shape
            tr = n_exp * rows // CAST_UNITS
            assert rows % tr == 0 and tr % 16 == 0
            per_expert = rows // tr

            def unit(i, f, per_expert=per_expert):
                u = jnp.minimum(i * nf + f, CAST_UNITS - 1)
                return u // per_expert, u % per_expert

            in_specs.append(pl.BlockSpec((None, None, tr, cols), lambda i, f, unit=unit: (cast_layer, *unit(i, f), 0)))
            out_specs.append(pl.BlockSpec((None, tr, cols), lambda i, f, unit=unit: (*unit(i, f), 0)))
            out_shape.append(jax.ShapeDtypeStruct((n_exp, rows, cols), BF16))
            cast_args.append(w)
    res = pl.pallas_call(
        functools.partial(_ffn_kernel, n_cast=len(cast_args)),
        grid=(m // TM_FFN, nf),
        in_specs=in_specs,
        out_specs=out_specs,
        out_shape=out_shape,
        scratch_shapes=[pltpu.VMEM((TM_FFN, d), BF16), pltpu.VMEM((TM_FFN, d), F32)],
        input_output_aliases={0: 0},
        compiler_params=_params("arbitrary", "arbitrary"),
        name="dense_swiglu",
    )(x, g2, mods, mods, wg, wu, wd, mods, *cast_args)
    return res[0] if cast is None else res


def _glu_kernel(x_ref, g_ref, sh_ref, sc_ref, wa_ref, wg_ref, ba_ref, bg_ref, z_ref, h_ref):
    @pl.when(pl.program_id(1) == 0)
    def _():
        _fill_modulated(x_ref, g_ref, sh_ref, sc_ref, h_ref)

    h = h_ref[...]
    a = jnp.dot(h, wa_ref[...], preferred_element_type=F32) + ba_ref[...]
    gt = jnp.dot(h, wg_ref[...], preferred_element_type=F32) + bg_ref[...]
    z_ref[...] = a * jax.nn.sigmoid(gt)


def _conv_glu(x, mods, layer, row_of, g1, w1, b1):
    m, d = x.shape
    nj = d // TN
    return pl.pallas_call(
        _glu_kernel,
        grid=(m // TM, nj),
        in_specs=[
            pl.BlockSpec((TM, d), lambda i, j: (i, 0)),
            pl.BlockSpec((1, d), lambda i, j: (0, 0)),
            _mod_spec(layer, 0, row_of),
            _mod_spec(layer, 1, row_of),
            pl.BlockSpec((d, TN), lambda i, j: (0, j)),
            pl.BlockSpec((d, TN), lambda i, j: (0, j + nj)),
            pl.BlockSpec((1, TN), lambda i, j: (0, j)),
            pl.BlockSpec((1, TN), lambda i, j: (0, j + nj)),
        ],
        out_specs=pl.BlockSpec((TM, TN), lambda i, j: (i, j)),
        out_shape=jax.ShapeDtypeStruct((m, d), F32),
        scratch_shapes=[pltpu.VMEM((TM, d), BF16)],
        compiler_params=_params("parallel", "arbitrary"),
        name="conv_glu",
    )(x, g1, mods, mods, w1, w1, b1, b1)


def _dwconv_kernel(z_ref, w_ref, b_ref, o_ref, pad_ref):
    seq, tc = z_ref.shape
    rc = 64
    zeros = jnp.zeros((CONV_PAD, tc), F32)
    pad_ref[0:CONV_PAD, :] = zeros
    pad_ref[CONV_PAD + seq:, :] = zeros

    def copy(c, carry):
        r0 = pl.multiple_of(c * rc, rc)
        pad_ref[pl.ds(CONV_PAD + r0, rc), :] = z_ref[pl.ds(r0, rc), :]
        return carry

    lax.fori_loop(0, seq // rc, copy, 0)
    bias = jnp.broadcast_to(b_ref[...], (rc, tc))
    first = CONV_PAD - CONV_W // 2
    win_rows = rc + 2 * CONV_PAD
    sub = 8

    def conv(c, carry):
        r0 = pl.multiple_of(c * rc, rc)
        win = pad_ref[pl.ds(r0, win_rows), :]
        acc = bias
        for s in range(sub):
            shifted = win if s == 0 else pltpu.roll(win, win_rows - s, 0)
            for k in range(CONV_W):
                off = first + k
                if off % sub == s:
                    a0 = off - s
                    acc = acc + shifted[a0:a0 + rc, :] * w_ref[k:k + 1, :]
        o_ref[pl.ds(r0, rc), :] = acc
        return carry

    lax.fori_loop(0, seq // rc, conv, 0)


def _depthwise_conv(z, batch, seq, dw, dw_b):
    d = z.shape[1]
    tc = 256
    out = pl.pallas_call(
        _dwconv_kernel,
        grid=(batch, d // tc),
        in_specs=[
            pl.BlockSpec((None, seq, tc), lambda b, c: (b, 0, c)),
            pl.BlockSpec((CONV_W, tc), lambda b, c: (0, c)),
            pl.BlockSpec((1, tc), lambda b, c: (0, c)),
        ],
        out_specs=pl.BlockSpec((None, seq, tc), lambda b, c: (b, 0, c)),
        out_shape=jax.ShapeDtypeStruct((batch, seq, d), F32),
        scratch_shapes=[pltpu.VMEM((seq + 2 * CONV_PAD, tc), F32)],
        compiler_params=_params("parallel", "parallel"),
        name="depthwise_conv",
    )(z.reshape(batch, seq, d), dw, dw_b)
    return out.reshape(batch * seq, d)


def _conv_out_kernel(z_ref, lng_ref, lnb_ref, w_ref, b_ref, x_ref, gate_ref, o_ref, h_ref):
    @pl.when(pl.program_id(1) == 0)
    def _():
        tiles = _lane_tiles(z_ref.shape[1])

        def fn(rows):
            mu = _row_mean(lambda cols: z_ref[rows, cols], tiles)
            var = _row_mean(lambda cols: jnp.square(z_ref[rows, cols] - mu), tiles)
            inv = lax.rsqrt(var + EPS)
            for cols in tiles:
                y = ((z_ref[rows, cols] - mu) * inv) * lng_ref[:, cols] + lnb_ref[:, cols]
                h_ref[rows, cols] = jax.nn.silu(y).astype(BF16)

        _row_loop(z_ref.shape[0], fn)

    acc = jnp.dot(h_ref[...], w_ref[...], preferred_element_type=F32) + b_ref[...]
    o_ref[...] = x_ref[...] + gate_ref[...] * acc


def _conv_out(zc, x, mods, layer, row_of, ln_g, ln_b, w2, b2):
    m, d = x.shape
    return pl.pallas_call(
        _conv_out_kernel,
        grid=(m // TM, d // TN),
        in_specs=[
            pl.BlockSpec((TM, d), lambda i, j: (i, 0)),
            pl.BlockSpec((1, d), lambda i, j: (0, 0)),
            pl.BlockSpec((1, d), lambda i, j: (0, 0)),
            pl.BlockSpec((d, TN), lambda i, j: (0, j)),
            pl.BlockSpec((1, TN), lambda i, j: (0, j)),
            pl.BlockSpec((TM, TN), lambda i, j: (i, j)),
            _mod_spec(layer, 2, row_of, TN, lambda j: j),
        ],
        out_specs=pl.BlockSpec((TM, TN), lambda i, j: (i, j)),
        out_shape=jax.ShapeDtypeStruct((m, d), F32),
        scratch_shapes=[pltpu.VMEM((TM, d), BF16)],
        input_output_aliases={5: 0},
        compiler_params=_params("parallel", "arbitrary"),
        name="conv_out_projection",
    )(zc, ln_g, ln_b, w2, b2, x, mods)


def _router_kernel(x_ref, g_ref, sh_ref, sc_ref, rw_ref, base_ref, h_ref, ri_ref, rg_ref, cnt_ref, run_ref, tri_ref):
    tm = x_ref.shape[0]

    @pl.when(pl.program_id(0) == 0)
    def _():
        run_ref[...] = base_ref[...]
        src = lax.broadcasted_iota(I32, (tm, tm), 0)
        dst = lax.broadcasted_iota(I32, (tm, tm), 1)
        tri_ref[...] = jnp.where(src < dst, 1.0, 0.0).astype(BF16)

    _fill_modulated(x_ref, g_ref, sh_ref, sc_ref, h_ref)
    logits = lax.dot_general(rw_ref[...], h_ref[...], _NT, precision=lax.Precision.HIGHEST,
                             preferred_element_type=F32)
    eid = lax.broadcasted_iota(I32, logits.shape, 0).astype(F32)
    none = float(N_EXPERTS)
    m1 = jnp.max(logits, axis=0, keepdims=True)
    i1 = jnp.min(jnp.where(logits == m1, eid, none), axis=0, keepdims=True)
    rest = jnp.where(eid == i1, -jnp.inf, logits)
    m2 = jnp.max(rest, axis=0, keepdims=True)
    i2 = jnp.min(jnp.where(rest == m2, eid, none), axis=0, keepdims=True)
    e2 = jnp.exp(m2 - m1)
    gate1 = 1.0 / (1.0 + e2)
    gate2 = e2 / (1.0 + e2)
    hot1 = eid == i1
    hot2 = eid == i2
    chosen = jnp.where(hot1, 1.0, jnp.where(hot2, 1.0, 0.0))
    before = jnp.dot(chosen.astype(BF16), tri_ref[...], preferred_element_type=F32) + run_ref[:, 0:1]
    rank1 = jnp.sum(jnp.where(hot1, before, 0.0), axis=0, keepdims=True)
    rank2 = jnp.sum(jnp.where(hot2, before, 0.0), axis=0, keepdims=True)
    ri_ref[...] = jnp.zeros(ri_ref.shape, I32)
    ri_ref[0:1, :] = i1.astype(I32)
    ri_ref[1:2, :] = i2.astype(I32)
    ri_ref[2:3, :] = rank1.astype(I32)
    ri_ref[3:4, :] = rank2.astype(I32)
    rg_ref[...] = jnp.zeros(rg_ref.shape, F32)
    rg_ref[0:1, :] = gate1
    rg_ref[1:2, :] = gate2
    run_ref[...] = run_ref[...] + jnp.sum(chosen, axis=1, keepdims=True)
    cnt_ref[...] = run_ref[...]


def _router(x, mods, layer, row_of, g2, rw_t, base):
    m, d = x.shape
    tm = TM_ROUTE
    return pl.pallas_call(
        _router_kernel,
        grid=(m // tm,),
        in_specs=[
            pl.BlockSpec((tm, d), lambda i: (i, 0)),
            pl.BlockSpec((1, d), lambda i: (0, 0)),
            _mod_spec(layer, 3, row_of),
            _mod_spec(layer, 4, row_of),
            pl.BlockSpec((N_EXPERTS, d), lambda i: (0, 0)),
            pl.BlockSpec((N_EXPERTS, LANE), lambda i: (0, 0)),
        ],
        out_specs=[
            pl.BlockSpec((tm, d), lambda i: (i, 0)),
            pl.BlockSpec((8, tm), lambda i: (0, i)),
            pl.BlockSpec((8, tm), lambda i: (0, i)),
            pl.BlockSpec((N_EXPERTS, LANE), lambda i: (0, 0)),
        ],
        out_shape=[
            jax.ShapeDtypeStruct((m, d), F32),
            jax.ShapeDtypeStruct((8, m), I32),
            jax.ShapeDtypeStruct((8, m), F32),
            jax.ShapeDtypeStruct((N_EXPERTS, LANE), F32),
        ],
        scratch_shapes=[pltpu.VMEM((N_EXPERTS, LANE), F32), pltpu.VMEM((tm, tm), BF16)],
        compiler_params=_params("arbitrary"),
        name="moe_router",
    )(x, g2, mods, mods, rw_t, base)


def _dispatch_kernel(pos_ref, h_ref, xs_in_ref, xs_ref, sem):
    del xs_in_ref
    tm = h_ref.shape[0]
    n_tok = pos_ref.shape[0] // 2
    t0 = pl.program_id(0) * tm

    def row_copy(t, k):
        dst = pos_ref[k * n_tok + t0 + t]
        return pltpu.make_async_copy(h_ref.at[pl.ds(t, 1)], xs_ref.at[pl.ds(dst, 1)], sem)

    def start(t, carry):
        row_copy(t, 0).start()
        row_copy(t, 1).start()
        return carry

    def wait(t, carry):
        row_copy(t, 0).wait()
        row_copy(t, 1).wait()
        return carry

    lax.fori_loop(0, tm, start, 0, unroll=4)
    lax.fori_loop(0, tm, wait, 0, unroll=4)


def _dispatch(pos_flat, h, xs):
    m, d = h.shape
    return pl.pallas_call(
        _dispatch_kernel,
        grid_spec=pltpu.PrefetchScalarGridSpec(
            num_scalar_prefetch=1,
            grid=(m // TM_MOVE,),
            in_specs=[
                pl.BlockSpec((TM_MOVE, d), lambda i, pos: (i, 0)),
                pl.BlockSpec(memory_space=pl.ANY),
            ],
            out_specs=pl.BlockSpec(memory_space=pl.ANY),
            scratch_shapes=[pltpu.SemaphoreType.DMA(())],
        ),
        out_shape=jax.ShapeDtypeStruct(xs.shape, xs.dtype),
        input_output_aliases={2: 0},
        compiler_params=_params("arbitrary"),
        name="moe_dispatch",
    )(pos_flat, h, xs)


def _expert_kernel(te_ref, nu_ref, xs_ref, wg_ref, wu_ref, wd_ref, ys_ref, xb_ref, acc_ref):
    t = pl.program_id(0)
    f = pl.program_id(1)
    last = pl.num_programs(1) - 1
    used = t < nu_ref[0]

    @pl.when(used & (f == 0))
    def _():
        xb_ref[...] = xs_ref[...].astype(BF16)

    @pl.when(used)
    def _():
        h = xb_ref[...]
        a = jnp.dot(h, wg_ref[...], preferred_element_type=F32)
        u = jnp.dot(h, wu_ref[...], preferred_element_type=F32)
        y = jnp.dot((jax.nn.silu(a) * u).astype(BF16), wd_ref[...], preferred_element_type=F32)

        @pl.when(f == 0)
        def _():
            acc_ref[...] = y

        @pl.when(f > 0)
        def _():
            acc_ref[...] += y

    @pl.when(used & (f == last))
    def _():
        ys_ref[...] = acc_ref[...]

    @pl.when(jnp.logical_not(used) & (f == last))
    def _():
        ys_ref[...] = jnp.zeros(ys_ref.shape, ys_ref.dtype)


def _experts(tile_expert, n_used, xs, wg, wu, wd):
    p, d = xs.shape
    ff = wg.shape[2]
    nf = ff // TF_MOE

    def row_idx(t, f, te, nu):
        return (jnp.minimum(t, nu[0] - 1), 0)

    def hid(t, f, nu):
        return jnp.where(t < nu[0], f, nf - 1)

    return pl.pallas_call(
        _expert_kernel,
        grid_spec=pltpu.PrefetchScalarGridSpec(
            num_scalar_prefetch=2,
            grid=(p // TM_FFN, nf),
            in_specs=[
                pl.BlockSpec((TM_FFN, d), row_idx),
                pl.BlockSpec((None, d, TF_MOE), lambda t, f, te, nu: (te[t], 0, hid(t, f, nu))),
                pl.BlockSpec((None, d, TF_MOE), lambda t, f, te, nu: (te[t], 0, hid(t, f, nu))),
                pl.BlockSpec((None, TF_MOE, d), lambda t, f, te, nu: (te[t], hid(t, f, nu), 0)),
            ],
            out_specs=pl.BlockSpec((TM_FFN, d), lambda t, f, te, nu: (t, 0)),
            scratch_shapes=[pltpu.VMEM((TM_FFN, d), BF16), pltpu.VMEM((TM_FFN, d), F32)],
        ),
        out_shape=jax.ShapeDtypeStruct((p, d), F32),
        compiler_params=_params("arbitrary", "arbitrary"),
        name="moe_experts",
    )(tile_expert, n_used, xs, wg, wu, wd)


def _combine_kernel(pos_ref, ys_ref, x_ref, gt_ref, gate_ref, o_ref, buf_ref, sem):
    tm = x_ref.shape[0]
    n_tok = pos_ref.shape[0] // 2
    t0 = pl.program_id(0) * tm

    def row_copy(t, k):
        src = pos_ref[k * n_tok + t0 + t]
        return pltpu.make_async_copy(ys_ref.at[pl.ds(src, 1)], buf_ref.at[k, pl.ds(t, 1)], sem)

    def start(t, carry):
        row_copy(t, 0).start()
        row_copy(t, 1).start()
        return carry

    def wait(t, carry):
        row_copy(t, 0).wait()
        row_copy(t, 1).wait()
        return carry

    lax.fori_loop(0, tm, start, 0, unroll=4)
    lax.fori_loop(0, tm, wait, 0, unroll=4)
    g1 = gt_ref[:, 0:1]
    g2 = gt_ref[:, 1:2]
    for cols in _lane_tiles(x_ref.shape[1]):
        y = g1 * buf_ref[0, :, cols] + g2 * buf_ref[1, :, cols]
        o_ref[:, cols] = x_ref[:, cols] + gate_ref[:, cols] * y


def _combine(pos_flat, ys, x, gates_t, mods, layer, row_of):
    m, d = x.shape
    return pl.pallas_call(
        _combine_kernel,
        grid_spec=pltpu.PrefetchScalarGridSpec(
            num_scalar_prefetch=1,
            grid=(m // TM_MOVE,),
            in_specs=[
                pl.BlockSpec(memory_space=pl.ANY),
                pl.BlockSpec((TM_MOVE, d), lambda i, pos: (i, 0)),
                pl.BlockSpec((TM_MOVE, 2), lambda i, pos: (i, 0)),
                _mod_spec(layer, 5, row_of),
            ],
            out_specs=pl.BlockSpec((TM_MOVE, d), lambda i, pos: (i, 0)),
            scratch_shapes=[pltpu.VMEM((2, TM_MOVE, d), F32), pltpu.SemaphoreType.DMA(())],
        ),
        out_shape=jax.ShapeDtypeStruct((m, d), F32),
        input_output_aliases={2: 0},
        compiler_params=_params("arbitrary"),
        name="moe_combine",
    )(pos_flat, ys, x, gates_t, mods)


def _moe(streams, mods, layer, g2, rw_t, wg, wu, wd):
    n_rows = sum(2 * x.shape[0] for x, _, _ in streams)
    n_tiles = n_rows // TM_FFN + N_EXPERTS
    base = jnp.zeros((N_EXPERTS, LANE), F32)
    routed = []
    for x, row_of, tile in streams:
        h, ri, rg, base = _router(x, mods, layer, lambda i, ro=row_of: ro(i, TM_ROUTE), g2, rw_t, base)
        routed.append((h, ri, rg))
    counts = base[:, 0].astype(I32)
    tiles_e = (counts + TM_FFN - 1) // TM_FFN
    tile_end = jnp.cumsum(tiles_e)
    row_off = (tile_end - tiles_e) * TM_FFN
    n_used = tile_end[-1:]
    tile_ids = jnp.minimum(jnp.arange(n_tiles, dtype=I32), n_used[0] - 1)
    tile_expert = jnp.sum(tile_ids[:, None] >= tile_end[None, :], axis=1).astype(I32)
    xs = jnp.zeros((n_tiles * TM_FFN, D_MODEL), F32)
    pos_all = []
    for h, ri, rg in routed:
        ids, rank = ri[0:2], ri[2:4]
        off = jnp.sum(jnp.where(ids[:, :, None] == jnp.arange(N_EXPERTS)[None, None, :], row_off[None, None, :], 0), axis=-1)
        pos = (off + rank).reshape(-1)
        pos_all.append(pos)
        xs = _dispatch(pos, h, xs)
    ys = _experts(tile_expert, n_used, xs, wg, wu, wd)
    outs = []
    for (x, row_of, tile), (h, ri, rg), pos in zip(streams, routed, pos_all):
        outs.append(_combine(pos, ys, x, rg[0:2].T, mods, layer, lambda i, ro=row_of: ro(i, TM_MOVE)))
    return outs


def kernel(x_prompt, x_sample, cache_k, cache_v, c, c_ctx, w_mod, b_mod, norm1_g, norm2_g, w_in_even, q_norm_g, k_norm_g, rpb, sgu_ln_g, sgu_ln_b, sgu_w, sgu_b, w_out_even, ffn_w_gate, ffn_w_up, ffn_w_down, conv_w1, conv_b1, conv_dw, conv_dw_b, conv_ln_g, conv_ln_b, conv_w2, conv_b2, router_w, moe_w_gate, moe_w_up, moe_w_down):
    batch, seq, d = x_prompt.shape
    dec_batch, dec_seq, _ = x_sample.shape
    depth = w_mod.shape[0]
    n_even = w_in_even.shape[0]
    past = cache_k.shape[2]
    xp = x_prompt.reshape(batch * seq, d)
    xs = x_sample.reshape(dec_batch * dec_seq, d)

    cvec = jnp.zeros((N_MOD_ROWS, d), F32).at[0].set(c_ctx).at[1:1 + dec_batch].set(c)
    mods = _modulation(cvec, w_mod, b_mod)

    prompt_row = lambda i, tile=TM: 0
    sample_row = lambda i, tile=TM: 1 + (i * tile) // dec_seq
    row2 = lambda a: a.reshape(1, -1)

    new_k, new_v = [], []
    for l in range(depth):
        g1, g2 = row2(norm1_g[l]), row2(norm2_g[l])
        if l % 2 == 0:
            e = l // 2
            w_in = _layer_bf16(w_in_even, e)
            w_out = _layer_bf16(w_out_even, e)
            w_s = sgu_w[e].astype(BF16)
            b_s_t = sgu_b[e].T
            ff_pad = D_FF_PAD - ffn_w_gate.shape[2]
            wg = jnp.pad(_layer_bf16(ffn_w_gate, e), ((0, 0), (0, ff_pad)))
            wu = jnp.pad(_layer_bf16(ffn_w_up, e), ((0, 0), (0, ff_pad)))
            wd = jnp.pad(_layer_bf16(ffn_w_down, e), ((0, ff_pad), (0, 0)))
            ck = cache_k[:, e].reshape(dec_batch, past, D_A).astype(BF16)
            cv = cache_v[:, e].reshape(dec_batch, past, D_A).astype(BF16)
            bias_tab = _bias_table(rpb[e])
            norm_args = (row2(q_norm_g[e]), row2(k_norm_g[e]), row2(sgu_ln_g[e]), row2(sgu_ln_b[e]))

            pp, kv = _even_projection(xp, mods, l, prompt_row, g1, w_in, *norm_args, with_kv=True)
            new_k.append(kv[:, :D_A].reshape(batch, seq, N_HEADS, HEAD_DIM))
            new_v.append(kv[:, D_A:].reshape(batch, seq, N_HEADS, HEAD_DIM))
            ap = _context_attention(pp, batch, seq)
            sp = _spatial_gating(pp, w_s, b_s_t)
            xp = _out_projection(xp, ap, sp, w_out, mods, l, prompt_row)

            ps, _ = _even_projection(xs, mods, l, sample_row, g1, w_in, *norm_args, with_kv=False)
            a_s = _neighbourhood_attention(ps, ck, cv, bias_tab, dec_batch, dec_seq)
            s_s = _spatial_gating(ps, w_s, b_s_t)
            xs = _out_projection(xs, a_s, s_s, w_out, mods, l, sample_row)

            xp = _dense_ffn(xp, mods, l, lambda i: prompt_row(i, TM_FFN), g2, wg, wu, wd)
            xs, *moe_bf16 = _dense_ffn(xs, mods, l, lambda i: sample_row(i, TM_FFN), g2, wg, wu, wd,
                                       cast=((moe_w_gate, moe_w_up, moe_w_down), e))
        else:
            o = l // 2
            w1 = _layer_bf16(conv_w1, o)
            w2 = _layer_bf16(conv_w2, o)
            b1, b2 = row2(conv_b1[o]), row2(conv_b2[o])
            dw, dw_b = conv_dw[o], row2(conv_dw_b[o])
            ln_g, ln_b = row2(conv_ln_g[o]), row2(conv_ln_b[o])
            outs = []
            for x, row_of, nb, ns in ((xp, prompt_row, batch, seq), (xs, sample_row, dec_batch, dec_seq)):
                z = _conv_glu(x, mods, l, row_of, g1, w1, b1)
                zc = _depthwise_conv(z, nb, ns, dw, dw_b)
                outs.append(_conv_out(zc, x, mods, l, row_of, ln_g, ln_b, w2, b2))
            xp, xs = outs
            xp, xs = _moe(
                [(xp, prompt_row, None), (xs, sample_row, None)], mods, l, g2,
                router_w[o].T, *moe_bf16)

    new_cache_k = jnp.stack(new_k, axis=1)
    new_cache_v = jnp.stack(new_v, axis=1)
    return (xp.reshape(batch, seq, d), xs.reshape(dec_batch, dec_seq, d), new_cache_k, new_cache_v)
```
